```python
import math
import jax, jax.numpy as jnp
from jax import lax
import numpy as np

D_MODEL = 2048
BATCH = 4
SEQ = 2048
DEPTH = 4
DEC_BATCH = 128
DEC_SEQ = 8
PAST_LEN = 16384
PAGE_SIZE = 128

MIX = D_MODEL
W_A = MIX // 2
H_A = 16
HD_A = W_A // H_A
CONV_A = 4
C_LRU = 8.0
W_B = MIX // 4
CONV_B = 31
W_C = MIX // 4
H_C = 8
HD_C = W_C // H_C
CHUNK = 128
EPS = 1e-6
SPLITS = (W_A, W_A, W_B, W_B, W_B, W_C, W_C, W_C)
IN_COLS = sum(SPLITS)

kernel_name = "hybrid_lru_conformer_gmlp_decoder_step"


def rmsnorm(x, g):
    xf = x.astype(jnp.float32)
    y = xf * lax.rsqrt(jnp.mean(xf * xf, axis=-1, keepdims=True) + EPS)
    return y.astype(x.dtype) * g


def layernorm(x, g, b):
    xf = x.astype(jnp.float32)
    mu = jnp.mean(xf, axis=-1, keepdims=True)
    xc = xf - mu
    y = xc * lax.rsqrt(jnp.mean(xc * xc, axis=-1, keepdims=True) + EPS)
    return y.astype(x.dtype) * g + b


def causal_dwconv(x, hist, w, b):
    xc = jnp.concatenate([hist.astype(x.dtype), x], axis=1)
    y = lax.conv_general_dilated(xc, w[:, None, :].astype(x.dtype), window_strides=(1,),
                                 padding='VALID', dimension_numbers=('NWC', 'WIO', 'NWC'),
                                 feature_group_count=x.shape[-1])
    return y + b, xc[:, -(w.shape[0] - 1):]


def rg_lru(x, h0, w_a, b_a, w_i, b_i, lam):
    B, L, _ = x.shape
    xh = x.reshape(B, L, H_A, HD_A)
    r = jax.nn.sigmoid(jnp.einsum('blhd,hde->blhe', xh, w_a).reshape(B, L, W_A) + b_a)
    i = jax.nn.sigmoid(jnp.einsum('blhd,hde->blhe', xh, w_i).reshape(B, L, W_A) + b_i)
    log_a = -C_LRU * r.astype(jnp.float32) * jax.nn.softplus(-lam.astype(jnp.float32))
    a = jnp.exp(log_a)
    bterm = jnp.sqrt(-jnp.expm1(2.0 * log_a)) * (i * x).astype(jnp.float32)
    bterm = bterm.at[:, 0].add(a[:, 0] * h0.astype(jnp.float32))

    def combine(left, right):
        a1, b1 = left
        a2, b2 = right
        return a1 * a2, a2 * b1 + b2

    _, h = lax.associative_scan(combine, (a, bterm), axis=1)
    return h.astype(x.dtype), h[:, -1].astype(h0.dtype)


def chunk_spatial_gate(u, vn, w_s, b_s):
    B, L, _ = u.shape
    n = -(-L // CHUNK)
    pad = n * CHUNK - L
    vp = jnp.pad(vn, ((0, 0), (0, pad), (0, 0))).reshape(B, n, CHUNK, H_C, HD_C)
    mask = jnp.tril(jnp.ones((CHUNK, CHUNK), dtype=bool))
    ws = jnp.where(mask[None], w_s, jnp.zeros_like(w_s))
    mixed = jnp.einsum('hts,bnshd->bnthd', ws, vp) + b_s.T[None, None, :, :, None]
    mixed = mixed.reshape(B, n * CHUNK, W_C)[:, :L]
    return u * mixed


def layer(x, c, hist_a, h0, hist_b,
          norm_g, w_ada, b_ada, w_in, conv_a_w, conv_a_b, lru_wa, lru_ba, lru_wi, lru_bi, lru_lam,
          conv_b_w, conv_b_b, ln_b_g, ln_b_b, ln_c_g, ln_c_b, gmlp_ws, gmlp_bs, w_out):
    mod = jax.nn.silu(c) @ w_ada + b_ada
    shift, scale, gate = jnp.split(mod[:, None, :], 3, axis=-1)
    xn = rmsnorm(x, norm_g) * (1.0 + scale) + shift
    z = xn @ w_in
    idx = list(np.cumsum(SPLITS)[:-1])
    xa, ga, gl_a, gl_b, gb, u, v, gc = jnp.split(z, idx, axis=-1)
    xa_c, new_hist_a = causal_dwconv(xa, hist_a, conv_a_w, conv_a_b)
    h, h_last = rg_lru(xa_c, h0, lru_wa, lru_ba, lru_wi, lru_bi, lru_lam)
    ya = h * jax.nn.silu(ga)
    glu = gl_a * jax.nn.sigmoid(gl_b)
    cb, new_hist_b = causal_dwconv(glu, hist_b, conv_b_w, conv_b_b)
    yb = jax.nn.silu(layernorm(cb, ln_b_g, ln_b_b)) * jax.nn.silu(gb)
    vn = layernorm(v, ln_c_g, ln_c_b)
    yc = chunk_spatial_gate(u, vn, gmlp_ws, gmlp_bs) * jax.nn.silu(gc)
    out = jnp.concatenate([ya, yb, yc], axis=-1) @ w_out
    return x + gate * out, new_hist_a, h_last, new_hist_b, vn


def setup_inputs(seed: int = 0) -> dict:
    key = jax.random.key(seed)
    ks = iter(jax.random.split(key, 40))
    f32 = jnp.float32
    nrm = lambda shape, s: jax.random.normal(next(ks), shape, f32) * s
    a0 = jax.random.uniform(next(ks), (DEPTH, W_A), f32, 0.9, 0.999)
    return {
        "x_prompt": nrm((BATCH, SEQ, D_MODEL), 1.0),
        "x_sample": nrm((DEC_BATCH, DEC_SEQ, D_MODEL), 1.0),
        "c_prompt": nrm((BATCH, D_MODEL), 1.0),
        "c_sample": nrm((DEC_BATCH, D_MODEL), 1.0),
        "state_lru_conv": nrm((DEPTH, DEC_BATCH, CONV_A - 1, W_A), 1.0),
        "state_lru_h": nrm((DEPTH, DEC_BATCH, W_A), 0.5),
        "state_ccm_conv": nrm((DEPTH, DEC_BATCH, CONV_B - 1, W_B), 0.5),
        "norm_g": 1.0 + nrm((DEPTH, D_MODEL), 0.02),
        "w_ada": nrm((DEPTH, D_MODEL, 3 * D_MODEL), 0.5 * D_MODEL ** -0.5),
        "b_ada": nrm((DEPTH, 3 * D_MODEL), 0.02),
        "w_in": nrm((DEPTH, D_MODEL, IN_COLS), D_MODEL ** -0.5),
        "conv_a_w": nrm((DEPTH, CONV_A, W_A), CONV_A ** -0.5),
        "conv_a_b": nrm((DEPTH, W_A), 0.02),
        "lru_wa": nrm((DEPTH, H_A, HD_A, HD_A), HD_A ** -0.5),
        "lru_ba": nrm((DEPTH, W_A), 0.02),
        "lru_wi": nrm((DEPTH, H_A, HD_A, HD_A), HD_A ** -0.5),
        "lru_bi": nrm((DEPTH, W_A), 0.02),
        "lru_lam": jnp.log(a0) - jnp.log1p(-a0),
        "conv_b_w": nrm((DEPTH, CONV_B, W_B), CONV_B ** -0.5),
        "conv_b_b": nrm((DEPTH, W_B), 0.02),
        "ln_b_g": 1.0 + nrm((DEPTH, W_B), 0.02),
        "ln_b_b": nrm((DEPTH, W_B), 0.02),
        "ln_c_g": 1.0 + nrm((DEPTH, W_C), 0.02),
        "ln_c_b": nrm((DEPTH, W_C), 0.02),
        "gmlp_ws": nrm((DEPTH, H_C, CHUNK, CHUNK), CHUNK ** -0.5),
        "gmlp_bs": 1.0 + nrm((DEPTH, H_C, CHUNK), 0.02),
        "w_out": nrm((DEPTH, MIX, D_MODEL), MIX ** -0.5),
        "final_g": 1.0 + nrm((D_MODEL,), 0.02),
    }


def reference(x_prompt, x_sample, c_prompt, c_sample, state_lru_conv, state_lru_h, state_ccm_conv,
              norm_g, w_ada, b_ada, w_in, conv_a_w, conv_a_b, lru_wa, lru_ba, lru_wi, lru_bi, lru_lam,
              conv_b_w, conv_b_b, ln_b_g, ln_b_b, ln_c_g, ln_c_b, gmlp_ws, gmlp_bs, w_out, final_g):
    bp = x_prompt.shape[0]
    dt = x_prompt.dtype
    xp, xs = x_prompt, x_sample
    conv_a_p, h_p, conv_b_p = [], [], []
    conv_a_s, h_s, conv_b_s, v_s = [], [], [], []
    for l in range(DEPTH):
        params = (norm_g[l], w_ada[l], b_ada[l], w_in[l], conv_a_w[l], conv_a_b[l], lru_wa[l], lru_ba[l],
                  lru_wi[l], lru_bi[l], lru_lam[l], conv_b_w[l], conv_b_b[l], ln_b_g[l], ln_b_b[l],
                  ln_c_g[l], ln_c_b[l], gmlp_ws[l], gmlp_bs[l], w_out[l])
        xp, ha, hl, hb, _ = layer(xp, c_prompt,
                                  jnp.zeros((bp, CONV_A - 1, W_A), dt), jnp.zeros((bp, W_A), dt),
                                  jnp.zeros((bp, CONV_B - 1, W_B), dt), *params)
        conv_a_p.append(ha); h_p.append(hl); conv_b_p.append(hb)
        xs, ha, hl, hb, vn = layer(xs, c_sample, state_lru_conv[l], state_lru_h[l], state_ccm_conv[l], *params)
        conv_a_s.append(ha); h_s.append(hl); conv_b_s.append(hb); v_s.append(vn)
    y_prompt = rmsnorm(xp, final_g)
    y_sample = rmsnorm(xs, final_g)
    return (y_prompt, y_sample,
            jnp.stack(conv_a_p), jnp.stack(h_p), jnp.stack(conv_b_p),
            jnp.stack(conv_a_s), jnp.stack(h_s), jnp.stack(conv_b_s), jnp.stack(v_s))
```

```python
import functools

import jax
import jax.numpy as jnp
from jax import lax
from jax.experimental import pallas as pl
from jax.experimental.pallas import tpu as pltpu

F32 = jnp.float32
BF16 = jnp.bfloat16

D_MODEL = 2048
DEPTH = 4
MIX = D_MODEL
W_A = MIX // 2
H_A = 16
HD_A = W_A // H_A
CONV_A = 4
C_LRU = 8.0
W_B = MIX // 4
CONV_B = 31
W_C = MIX // 4
H_C = 8
HD_C = W_C // H_C
CHUNK = 128
EPS = 1e-6
IN_COLS = 2 * W_A + 3 * W_B + 3 * W_C

O_XA = 0
O_GA = O_XA + W_A
O_GLA = O_GA + W_A
O_GLB = O_GLA + W_B
O_GB = O_GLB + W_B
O_U = O_GB + W_B
O_V = O_U + W_C
O_GC = O_V + W_C

SUBLANES = 8
LANES = 128
GATE_BLK = 256
N_GATE_BLK = W_A // GATE_BLK
HIST_A = CONV_A - 1
HIST_B = CONV_B - 1
PAD_A = SUBLANES
PAD_B = 32

VMEM_LIMIT = 58 * 1024 * 1024


def _dot(a, b):
    return jnp.dot(a, b, preferred_element_type=F32)


def _sigmoid(x):
    return 0.5 * jnp.tanh(0.5 * x) + 0.5


def _silu(x):
    return x * _sigmoid(x)


def _layernorm(x, g, b):
    mu = jnp.mean(x, axis=-1, keepdims=True)
    xc = x - mu
    return xc * lax.rsqrt(jnp.mean(xc * xc, axis=-1, keepdims=True) + EPS) * g + b


def _norm_mod(x, g, scale, shift):
    y = x * lax.rsqrt(jnp.mean(x * x, axis=-1, keepdims=True) + EPS)
    return (y * g) * (1.0 + scale) + shift


def _lru_coeffs(xac, wg_ref, ba, bi, lam):
    xacb = xac.astype(BF16)
    pa, pi = [], []
    for j in range(N_GATE_BLK):
        g = _dot(xacb[:, j * GATE_BLK:(j + 1) * GATE_BLK], wg_ref[j])
        pa.append(g[:, :GATE_BLK])
        pi.append(g[:, GATE_BLK:])
    r = _sigmoid(jnp.concatenate(pa, axis=1) + ba)
    i = _sigmoid(jnp.concatenate(pi, axis=1) + bi)
    log_a = (-C_LRU * jax.nn.softplus(-lam)) * r
    a = jnp.exp(log_a)
    coef = jnp.sqrt(-jnp.tanh(log_a) * (1.0 + a * a))
    return a, coef * (i * xac)


def _group_scan(a3, b3):
    row = lax.broadcasted_iota(jnp.int32, (1,) + a3.shape[1:], 1)
    for s in (1, 2, 4):
        keep = row >= s
        a_prev = jnp.where(keep, pltpu.roll(a3, s, axis=1), 1.0)
        b_prev = jnp.where(keep, pltpu.roll(b3, s, axis=1), 0.0)
        b3 = a3 * b_prev + b3
        a3 = a3 * a_prev
    return a3, b3


def _tril_mask(n):
    r = lax.broadcasted_iota(jnp.int32, (n, n), 0)
    c = lax.broadcasted_iota(jnp.int32, (n, n), 1)
    return c <= r


def _ada_kernel(cp_ref, cs_ref, w_ref, b_ref, mp_ref, ms_ref):
    w = w_ref[0].astype(BF16)
    b = b_ref[0]
    mp_ref[0] = _dot(_silu(cp_ref[...]).astype(BF16), w) + b
    ms_ref[0] = _dot(_silu(cs_ref[...]).astype(BF16), w) + b


def _ada_call(c_prompt, c_sample, w_ada, b_ada, tn=1024):
    bp, bs = c_prompt.shape[0], c_sample.shape[0]
    n3 = w_ada.shape[-1]
    return pl.pallas_call(
        _ada_kernel,
        grid=(DEPTH, n3 // tn),
        in_specs=[
            pl.BlockSpec((bp, D_MODEL), lambda l, j: (0, 0)),
            pl.BlockSpec((bs, D_MODEL), lambda l, j: (0, 0)),
            pl.BlockSpec((1, D_MODEL, tn), lambda l, j: (l, 0, j)),
            pl.BlockSpec((1, 1, tn), lambda l, j: (l, 0, j)),
        ],
        out_specs=[
            pl.BlockSpec((1, bp, tn), lambda l, j: (l, 0, j)),
            pl.BlockSpec((1, bs, tn), lambda l, j: (l, 0, j)),
        ],
        out_shape=[
            jax.ShapeDtypeStruct((DEPTH, bp, n3), F32),
            jax.ShapeDtypeStruct((DEPTH, bs, n3), F32),
        ],
        compiler_params=pltpu.CompilerParams(
            dimension_semantics=("arbitrary", "arbitrary"), vmem_limit_bytes=VMEM_LIMIT),
        name="ada_mod",
    )(c_prompt, c_sample, w_ada, b_ada.reshape(DEPTH, 1, n3))


def _prompt_kernel(x_ref, mod_ref, ng_ref, win_ref, wout_ref, caw_ref, cab_ref, wg_ref, ba_ref, bi_ref,
                   lam_ref, cbw_ref, cbb_ref, lnbg_ref, lnbb_ref, lncg_ref, lncb_ref, ws_ref, bs_ref,
                   xo_ref, hista_ref, hlast_ref, histb_ref,
                   xa_buf, glu_buf, hc_ref, y_buf, *, tl):
    c = pl.program_id(1)

    @pl.when(c == 0)
    def _():
        xa_buf[0:PAD_A, :] = jnp.zeros((PAD_A, W_A), F32)
        glu_buf[0:PAD_B, :] = jnp.zeros((PAD_B, W_B), F32)
        hc_ref[...] = jnp.zeros_like(hc_ref)

    x = x_ref[0]
    shift = mod_ref[0, :, 0:D_MODEL]
    scale = mod_ref[0, :, D_MODEL:2 * D_MODEL]
    gate = mod_ref[0, :, 2 * D_MODEL:3 * D_MODEL]
    xnb = _norm_mod(x, ng_ref[...], scale, shift).astype(BF16)

    xa_buf[PAD_A:PAD_A + tl, :] = _dot(xnb, win_ref[:, O_XA:O_XA + W_A])
    xac = cab_ref[...]
    for k in range(CONV_A):
        o = PAD_A - HIST_A + k
        xac = xac + caw_ref[k:k + 1, :] * xa_buf[o:o + tl, :]
    new_hist_a = xa_buf[PAD_A + tl - HIST_A:PAD_A + tl, :]
    hista_ref[0] = new_hist_a
    xa_buf[PAD_A - HIST_A:PAD_A, :] = new_hist_a

    a, b = _lru_coeffs(xac, wg_ref, ba_ref[...], bi_ref[...], lam_ref[...])
    ng = tl // SUBLANES
    a3, b3 = _group_scan(a.reshape(ng, SUBLANES, W_A), b.reshape(ng, SUBLANES, W_A))
    carry = hc_ref[...]
    hs = []
    for g in range(ng):
        hg = b3[g] + a3[g] * carry
        carry = hg[SUBLANES - 1:SUBLANES, :]
        hs.append(hg)
    h = jnp.concatenate(hs, axis=0)
    hc_ref[...] = carry
    hlast_ref[0] = carry
    ga = _dot(xnb, win_ref[:, O_GA:O_GA + W_A])
    y_buf[:, 0:W_A] = (h * _silu(ga)).astype(BF16)

    gla = _dot(xnb, win_ref[:, O_GLA:O_GLA + W_B])
    glb = _dot(xnb, win_ref[:, O_GLB:O_GLB + W_B])
    glu_buf[PAD_B:PAD_B + tl, :] = gla * _sigmoid(glb)
    cb = cbb_ref[...]
    for k in range(CONV_B):
        o = PAD_B - HIST_B + k
        cb = cb + cbw_ref[k:k + 1, :] * glu_buf[o:o + tl, :]
    new_hist_b = glu_buf[PAD_B + tl - HIST_B:PAD_B + tl, :]
    histb_ref[0] = new_hist_b
    glu_buf[PAD_B - HIST_B:PAD_B, :] = new_hist_b
    gb = _dot(xnb, win_ref[:, O_GB:O_GB + W_B])
    yb = _silu(_layernorm(cb, lnbg_ref[...], lnbb_ref[...])) * _silu(gb)
    y_buf[:, W_A:W_A + W_B] = yb.astype(BF16)

    v = _dot(xnb, win_ref[:, O_V:O_V + W_C])
    vnb = _layernorm(v, lncg_ref[...], lncb_ref[...]).astype(BF16)
    tri = _tril_mask(CHUNK)
    lane = lax.broadcasted_iota(jnp.int32, (CHUNK, LANES), 1)
    lo_half = lane < HD_C
    wpair = []
    for p in range(H_C // 2):
        w0 = jnp.where(tri, ws_ref[2 * p], 0.0).astype(BF16)
        w1 = jnp.where(tri, ws_ref[2 * p + 1], 0.0).astype(BF16)
        wpair.append(jnp.concatenate([w0, w1], axis=1))
    zero = jnp.zeros((CHUNK, LANES), BF16)
    chunks = []
    for cc in range(tl // CHUNK):
        cols = []
        for p in range(H_C // 2):
            vp = vnb[cc * CHUNK:(cc + 1) * CHUNK, p * LANES:(p + 1) * LANES]
            rhs = jnp.concatenate([jnp.where(lo_half, vp, zero), jnp.where(lo_half, zero, vp)], axis=0)
            cols.append(_dot(wpair[p], rhs))
        chunks.append(jnp.concatenate(cols, axis=1) + bs_ref[...])
    mixed = jnp.concatenate(chunks, axis=0)
    u = _dot(xnb, win_ref[:, O_U:O_U + W_C])
    gc = _dot(xnb, win_ref[:, O_GC:O_GC + W_C])
    y_buf[:, W_A + W_B:MIX] = (u * mixed * _silu(gc)).astype(BF16)

    out = _dot(y_buf[...], wout_ref[...])
    xo_ref[0] = x + gate * out


def _const_spec(shape):
    nd = len(shape)
    return pl.BlockSpec(shape, lambda *_: (0,) * nd, pipeline_mode=pl.Buffered(1))


def _layer_param_specs():
    return [
        _const_spec((1, D_MODEL)),
        _const_spec((D_MODEL, IN_COLS)),
        _const_spec((MIX, D_MODEL)),
        _const_spec((CONV_A, W_A)),
        _const_spec((1, W_A)),
        _const_spec((N_GATE_BLK, GATE_BLK, 2 * GATE_BLK)),
        _const_spec((1, W_A)),
        _const_spec((1, W_A)),
        _const_spec((1, W_A)),
        _const_spec((CONV_B, W_B)),
        _const_spec((1, W_B)),
        _const_spec((1, W_B)),
        _const_spec((1, W_B)),
        _const_spec((1, W_C)),
        _const_spec((1, W_C)),
    ]


def _prompt_layer(x, mod, p, tl=256):
    bp, seq, _ = x.shape
    kern = functools.partial(_prompt_kernel, tl=tl)
    in_specs = [
        pl.BlockSpec((1, tl, D_MODEL), lambda b, c: (b, c, 0)),
        pl.BlockSpec((1, 1, 3 * D_MODEL), lambda b, c: (b, 0, 0)),
    ] + _layer_param_specs() + [
        _const_spec((H_C, CHUNK, CHUNK)),
        _const_spec((CHUNK, W_C)),
    ]
    out_specs = [
        pl.BlockSpec((1, tl, D_MODEL), lambda b, c: (b, c, 0)),
        pl.BlockSpec((1, HIST_A, W_A), lambda b, c: (b, 0, 0)),
        pl.BlockSpec((1, 1, W_A), lambda b, c: (b, 0, 0)),
        pl.BlockSpec((1, HIST_B, W_B), lambda b, c: (b, 0, 0)),
    ]
    out_shape = [
        jax.ShapeDtypeStruct((bp, seq, D_MODEL), F32),
        jax.ShapeDtypeStruct((bp, HIST_A, W_A), F32),
        jax.ShapeDtypeStruct((bp, 1, W_A), F32),
        jax.ShapeDtypeStruct((bp, HIST_B, W_B), F32),
    ]
    return pl.pallas_call(
        kern,
        grid=(bp, seq // tl),
        in_specs=in_specs,
        out_specs=out_specs,
        out_shape=out_shape,
        scratch_shapes=[
            pltpu.VMEM((PAD_A + tl, W_A), F32),
            pltpu.VMEM((PAD_B + tl, W_B), F32),
            pltpu.VMEM((1, W_A), F32),
            pltpu.VMEM((tl, MIX), BF16),
        ],
        compiler_params=pltpu.CompilerParams(
            dimension_semantics=("arbitrary", "arbitrary"), vmem_limit_bytes=VMEM_LIMIT),
        name="prompt_layer",
    )(x, mod.reshape(bp, 1, 3 * D_MODEL), *p["common"], p["ws"], p["bs_full"])


def _rows_to_tokens(ref, nb, lo, hi):
    return jnp.concatenate(
        [jnp.broadcast_to(ref[i:i + 1, lo:hi], (SUBLANES, hi - lo)) for i in range(nb)], axis=0)


def _sample_kernel(x_ref, mod_ref, ng_ref, win_ref, wout_ref, caw_ref, cab_ref, wg_ref, ba_ref, bi_ref,
                   lam_ref, cbw_ref, cbb_ref, lnbg_ref, lnbb_ref, lncg_ref, lncb_ref, wcol_ref, bs8_ref,
                   hista_in, h0_in, histb_in,
                   xo_ref, hista_out, hlast_out, histb_out, vn_out,
                   xa_buf, glu_buf, y_buf, *, nb):
    t = nb * SUBLANES
    x = x_ref[...].reshape(t, D_MODEL)
    shift = _rows_to_tokens(mod_ref, nb, 0, D_MODEL)
    scale = _rows_to_tokens(mod_ref, nb, D_MODEL, 2 * D_MODEL)
    xnb = _norm_mod(x, ng_ref[...], scale, shift).astype(BF16)

    xa_buf[:, PAD_A:PAD_A + SUBLANES, :] = _dot(xnb, win_ref[:, O_XA:O_XA + W_A]).reshape(nb, SUBLANES, W_A)
    xa_buf[:, PAD_A - HIST_A:PAD_A, :] = hista_in[...]
    xac3 = cab_ref[...]
    for k in range(CONV_A):
        o = PAD_A - HIST_A + k
        xac3 = xac3 + caw_ref[k:k + 1, :] * xa_buf[:, o:o + SUBLANES, :]
    hista_out[...] = xa_buf[:, PAD_A + SUBLANES - HIST_A:PAD_A + SUBLANES, :]
    a, b = _lru_coeffs(xac3.reshape(t, W_A), wg_ref, ba_ref[...], bi_ref[...], lam_ref[...])
    a3, b3 = _group_scan(a.reshape(nb, SUBLANES, W_A), b.reshape(nb, SUBLANES, W_A))
    h3 = b3 + a3 * h0_in[...]
    hlast_out[...] = h3[:, SUBLANES - 1:SUBLANES, :]
    ga = _dot(xnb, win_ref[:, O_GA:O_GA + W_A])
    y_buf[:, 0:W_A] = (h3.reshape(t, W_A) * _silu(ga)).astype(BF16)

    gla = _dot(xnb, win_ref[:, O_GLA:O_GLA + W_B])
    glb = _dot(xnb, win_ref[:, O_GLB:O_GLB + W_B])
    glu_buf[:, PAD_B:PAD_B + SUBLANES, :] = (gla * _sigmoid(glb)).reshape(nb, SUBLANES, W_B)
    glu_buf[:, PAD_B - HIST_B:PAD_B, :] = histb_in[...]
    cb3 = cbb_ref[...]
    for k in range(CONV_B):
        o = PAD_B - HIST_B + k
        cb3 = cb3 + cbw_ref[k:k + 1, :] * glu_buf[:, o:o + SUBLANES, :]
    histb_out[...] = glu_buf[:, PAD_B + SUBLANES - HIST_B:PAD_B + SUBLANES, :]
    gb = _dot(xnb, win_ref[:, O_GB:O_GB + W_B])
    yb = _silu(_layernorm(cb3.reshape(t, W_B), lnbg_ref[...], lnbb_ref[...])) * _silu(gb)
    y_buf[:, W_A:W_A + W_B] = yb.astype(BF16)

    v = _dot(xnb, win_ref[:, O_V:O_V + W_C])
    vn3 = _layernorm(v, lncg_ref[...], lncb_ref[...]).reshape(nb, SUBLANES, W_C)
    vn_out[...] = vn3
    trow = lax.broadcasted_iota(jnp.int32, (SUBLANES, W_C), 0)
    mixed3 = jnp.broadcast_to(bs8_ref[...], (nb, SUBLANES, W_C))
    for s in range(SUBLANES):
        wc = jnp.where(trow >= s, wcol_ref[s], 0.0)
        mixed3 = mixed3 + wc * vn3[:, s:s + 1, :]
    u = _dot(xnb, win_ref[:, O_U:O_U + W_C])
    gc = _dot(xnb, win_ref[:, O_GC:O_GC + W_C])
    y_buf[:, W_A + W_B:MIX] = (u * mixed3.reshape(t, W_C) * _silu(gc)).astype(BF16)

    out = _dot(y_buf[...], wout_ref[...])
    gate = _rows_to_tokens(mod_ref, nb, 2 * D_MODEL, 3 * D_MODEL)
    xo_ref[...] = (x + gate * out).reshape(nb, SUBLANES, D_MODEL)


def _sample_layer(x, mod, hist_a, h0, hist_b, p, nb=16):
    bs, ls, _ = x.shape
    assert ls == SUBLANES
    kern = functools.partial(_sample_kernel, nb=nb)
    seq_spec = lambda rows, w: pl.BlockSpec((nb, rows, w), lambda i: (i, 0, 0))
    in_specs = [
        seq_spec(SUBLANES, D_MODEL),
        pl.BlockSpec((nb, 3 * D_MODEL), lambda i: (i, 0)),
    ] + _layer_param_specs() + [
        _const_spec((SUBLANES, SUBLANES, W_C)),
        _const_spec((SUBLANES, W_C)),
        seq_spec(HIST_A, W_A),
        seq_spec(1, W_A),
        seq_spec(HIST_B, W_B),
    ]
    out_specs = [
        seq_spec(SUBLANES, D_MODEL),
        seq_spec(HIST_A, W_A),
        seq_spec(1, W_A),
        seq_spec(HIST_B, W_B),
        seq_spec(SUBLANES, W_C),
    ]
    out_shape = [
        jax.ShapeDtypeStruct((bs, SUBLANES, D_MODEL), F32),
        jax.ShapeDtypeStruct((bs, HIST_A, W_A), F32),
        jax.ShapeDtypeStruct((bs, 1, W_A), F32),
        jax.ShapeDtypeStruct((bs, HIST_B, W_B), F32),
        jax.ShapeDtypeStruct((bs, SUBLANES, W_C), F32),
    ]
    return pl.pallas_call(
        kern,
        grid=(bs // nb,),
        in_specs=in_specs,
        out_specs=out_specs,
        out_shape=out_shape,
        scratch_shapes=[
            pltpu.VMEM((nb, PAD_A + SUBLANES, W_A), F32),
            pltpu.VMEM((nb, PAD_B + SUBLANES, W_B), F32),
            pltpu.VMEM((nb * SUBLANES, MIX), BF16),
        ],
        compiler_params=pltpu.CompilerParams(
            dimension_semantics=("arbitrary",), vmem_limit_bytes=VMEM_LIMIT),
        name="sample_layer",
    )(x, mod, *p["common"], p["wcol"], p["bs8"], hist_a, h0.reshape(bs, 1, W_A), hist_b)


def _final_norm_kernel(x_ref, g_ref, o_ref):
    x = x_ref[...]
    o_ref[...] = (x * lax.rsqrt(jnp.mean(x * x, axis=-1, keepdims=True) + EPS)) * g_ref[...]


def _final_norm(x2d, g, tr=512):
    rows = x2d.shape[0]
    return pl.pallas_call(
        _final_norm_kernel,
        grid=(rows // tr,),
        in_specs=[pl.BlockSpec((tr, D_MODEL), lambda i: (i, 0)),
                  pl.BlockSpec((1, D_MODEL), lambda i: (0, 0))],
        out_specs=pl.BlockSpec((tr, D_MODEL), lambda i: (i, 0)),
        out_shape=jax.ShapeDtypeStruct(x2d.shape, F32),
        compiler_params=pltpu.CompilerParams(dimension_semantics=("arbitrary",)),
        name="final_norm",
    )(x2d, g.reshape(1, D_MODEL))


def _block_diag_gates(wa, wi):
    per = GATE_BLK // HD_A
    eye = jnp.eye(per, dtype=wa.dtype)

    def bd(w):
        w5 = w.reshape(N_GATE_BLK, per, HD_A, HD_A)
        return jnp.einsum("jpde,pq->jpdqe", w5, eye).reshape(N_GATE_BLK, GATE_BLK, GATE_BLK)

    return jnp.concatenate([bd(wa), bd(wi)], axis=-1).astype(BF16)


def _layer_params(l, norm_g, w_in, conv_a_w, conv_a_b, lru_wa, lru_ba, lru_wi, lru_bi, lru_lam, conv_b_w,
                  conv_b_b, ln_b_g, ln_b_b, ln_c_g, ln_c_b, gmlp_ws, gmlp_bs, w_out):
    row = lambda v: v[l].reshape(1, -1)
    common = (
        row(norm_g), w_in[l].astype(BF16), w_out[l].astype(BF16), conv_a_w[l], row(conv_a_b),
        _block_diag_gates(lru_wa[l], lru_wi[l]), row(lru_ba), row(lru_bi), row(lru_lam),
        conv_b_w[l], row(conv_b_b), row(ln_b_g), row(ln_b_b), row(ln_c_g), row(ln_c_b),
    )
    bs_full = jnp.repeat(gmlp_bs[l].T, HD_C, axis=1)
    wcol = jnp.repeat(jnp.transpose(gmlp_ws[l][:, :SUBLANES, :SUBLANES], (2, 1, 0)), HD_C, axis=2)
    return {"common": common, "ws": gmlp_ws[l], "bs_full": bs_full, "wcol": wcol,
            "bs8": bs_full[:SUBLANES]}


def kernel(x_prompt, x_sample, c_prompt, c_sample, state_lru_conv, state_lru_h, state_ccm_conv, norm_g, w_ada,
           b_ada, w_in, conv_a_w, conv_a_b, lru_wa, lru_ba, lru_wi, lru_bi, lru_lam, conv_b_w, conv_b_b, ln_b_g,
           ln_b_b, ln_c_g, ln_c_b, gmlp_ws, gmlp_bs, w_out, final_g):
    bp, seq, _ = x_prompt.shape
    bs, ls, _ = x_sample.shape
    mod_p, mod_s = _ada_call(c_prompt, c_sample, w_ada, b_ada)
    xp, xs = x_prompt, x_sample
    conv_a_p, h_p, conv_b_p = [], [], []
    conv_a_s, h_s, conv_b_s, v_s = [], [], [], []
    for l in range(DEPTH):
        p = _layer_params(l, norm_g, w_in, conv_a_w, conv_a_b, lru_wa, lru_ba, lru_wi, lru_bi, lru_lam,
                          conv_b_w, conv_b_b, ln_b_g, ln_b_b, ln_c_g, ln_c_b, gmlp_ws, gmlp_bs, w_out)
        xp, ha, hl, hb = _prompt_layer(xp, mod_p[l], p)
        conv_a_p.append(ha); h_p.append(hl.reshape(bp, W_A)); conv_b_p.append(hb)
        xs, ha, hl, hb, vn = _sample_layer(xs, mod_s[l], state_lru_conv[l], state_lru_h[l],
                                           state_ccm_conv[l], p)
        conv_a_s.append(ha); h_s.append(hl.reshape(bs, W_A)); conv_b_s.append(hb); v_s.append(vn)
    y_prompt = _final_norm(xp.reshape(bp * seq, D_MODEL), final_g).reshape(bp, seq, D_MODEL)
    y_sample = _final_norm(xs.reshape(bs * ls, D_MODEL), final_g).reshape(bs, ls, D_MODEL)
    return (y_prompt, y_sample,
            jnp.stack(conv_a_p), jnp.stack(h_p), jnp.stack(conv_b_p),
            jnp.stack(conv_a_s), jnp.stack(h_s), jnp.stack(conv_b_s), jnp.stack(v_s))
```

```python
import functools

import jax
import jax.numpy as jnp
from jax import lax
from jax.experimental import pallas as pl
from jax.experimental.pallas import tpu as pltpu

F32 = jnp.float32
BF16 = jnp.bfloat16

D_MODEL = 2048
DEPTH = 4
MIX = D_MODEL
W_A = MIX // 2
H_A = 16
HD_A = W_A // H_A
CONV_A = 4
C_LRU = 8.0
W_B = MIX // 4
CONV_B = 31
W_C = MIX // 4
H_C = 8
HD_C = W_C // H_C
CHUNK = 128
EPS = 1e-6
IN_COLS = 2 * W_A + 3 * W_B + 3 * W_C

O_XA = 0
O_GA = O_XA + W_A
O_GLA = O_GA + W_A
O_GLB = O_GLA + W_B
O_GB = O_GLB + W_B
O_U = O_GB + W_B
O_V = O_U + W_C
O_GC = O_V + W_C

SUBLANES = 8
LANES = 128
GATE_BLK = 256
N_GATE_BLK = W_A // GATE_BLK
HIST_A = CONV_A - 1
HIST_B = CONV_B - 1
PAD_A = SUBLANES
PAD_B = 32
SLAB_ROWS = 128
ROW_BLK = 64
SEQ_BLK = 4

VMEM_LIMIT = 58 * 1024 * 1024


def _dot(a, b):
    return jnp.dot(a, b, preferred_element_type=F32)


def _slab_copy(w_hbm, l, s, width, stage, sem):
    slot = s % 2
    return pltpu.make_async_copy(
        w_hbm.at[l, pl.ds(s * SLAB_ROWS, SLAB_ROWS), :], stage.at[slot, :, 0:width], sem.at[slot])


def _load_cast_weights(w_hbm, l, dst, stage, sem):
    rows, width = dst.shape
    n_slabs = rows // SLAB_ROWS
    _slab_copy(w_hbm, l, 0, width, stage, sem).start()

    def body(s, carry):
        @pl.when(s + 1 < n_slabs)
        def _():
            _slab_copy(w_hbm, l, s + 1, width, stage, sem).start()

        _slab_copy(w_hbm, l, s, width, stage, sem).wait()
        r0 = pl.multiple_of(s * SLAB_ROWS, SLAB_ROWS)
        dst[pl.ds(r0, SLAB_ROWS), :] = stage[s % 2, :, 0:width].astype(BF16)
        return carry

    lax.fori_loop(0, n_slabs, body, 0)


def _load_layer_weights(l, win_hbm, wout_hbm, win_s, wout_s, stage, sem):
    _load_cast_weights(win_hbm, l, win_s, stage, sem)
    _load_cast_weights(wout_hbm, l, wout_s, stage, sem)


def _sigmoid(x):
    return 0.5 * jnp.tanh(0.5 * x) + 0.5


def _silu(x):
    return x * _sigmoid(x)


def _layernorm(x, g, b):
    mu = jnp.mean(x, axis=-1, keepdims=True)
    xc = x - mu
    return xc * lax.rsqrt(jnp.mean(xc * xc, axis=-1, keepdims=True) + EPS) * g + b


def _rmsnorm(x, g):
    return (x * lax.rsqrt(jnp.mean(x * x, axis=-1, keepdims=True) + EPS)) * g


def _norm_mod(x, g, scale, shift):
    return _rmsnorm(x, g) * (1.0 + scale) + shift


def _lru_coeffs(xac, wg_ref, ba, bi, lam):
    xacb = xac.astype(BF16)
    pa, pi = [], []
    for j in range(N_GATE_BLK):
        g = _dot(xacb[:, j * GATE_BLK:(j + 1) * GATE_BLK], wg_ref[j])
        pa.append(g[:, :GATE_BLK])
        pi.append(g[:, GATE_BLK:])
    r = _sigmoid(jnp.concatenate(pa, axis=1) + ba)
    i = _sigmoid(jnp.concatenate(pi, axis=1) + bi)
    log_a = (-C_LRU * jax.nn.softplus(-lam)) * r
    a = jnp.exp(log_a)
    coef = jnp.sqrt(-jnp.tanh(log_a) * (1.0 + a * a))
    return a, coef * (i * xac)


def _group_scan(a3, b3):
    row = lax.broadcasted_iota(jnp.int32, (1,) + a3.shape[1:], 1)
    for s in (1, 2, 4):
        keep = row >= s
        a_prev = jnp.where(keep, pltpu.roll(a3, s, axis=1), 1.0)
        b_prev = jnp.where(keep, pltpu.roll(b3, s, axis=1), 0.0)
        b3 = a3 * b_prev + b3
        a3 = a3 * a_prev
    return a3, b3


def _tril_mask(n):
    r = lax.broadcasted_iota(jnp.int32, (n, n), 0)
    c = lax.broadcasted_iota(jnp.int32, (n, n), 1)
    return c <= r


def _win(win_ref, off, width):
    return win_ref[:, off:off + width]


def _ada_kernel(cp_ref, cs_ref, w_ref, b_ref, mp_ref, ms_ref):
    w = w_ref[0].astype(BF16)
    b = b_ref[0]
    mp_ref[0] = _dot(_silu(cp_ref[...]).astype(BF16), w) + b
    ms_ref[0] = _dot(_silu(cs_ref[...]).astype(BF16), w) + b


def _ada_call(c_prompt, c_sample, w_ada, b_ada, tn=1024):
    bp, bs = c_prompt.shape[0], c_sample.shape[0]
    n3 = w_ada.shape[-1]
    return pl.pallas_call(
        _ada_kernel,
        grid=(DEPTH, n3 // tn),
        in_specs=[
            pl.BlockSpec((bp, D_MODEL), lambda l, j: (0, 0)),
            pl.BlockSpec((bs, D_MODEL), lambda l, j: (0, 0)),
            pl.BlockSpec((1, D_MODEL, tn), lambda l, j: (l, 0, j)),
            pl.BlockSpec((1, 1, tn), lambda l, j: (l, 0, j)),
        ],
        out_specs=[
            pl.BlockSpec((1, bp, tn), lambda l, j: (l, 0, j)),
            pl.BlockSpec((1, bs, tn), lambda l, j: (l, 0, j)),
        ],
        out_shape=[
            jax.ShapeDtypeStruct((DEPTH, bp, n3), F32),
            jax.ShapeDtypeStruct((DEPTH, bs, n3), F32),
        ],
        compiler_params=pltpu.CompilerParams(
            dimension_semantics=("arbitrary", "arbitrary"), vmem_limit_bytes=VMEM_LIMIT),
        name="ada_mod",
    )(c_prompt, c_sample, w_ada, b_ada.reshape(DEPTH, 1, n3))


def _prompt_kernel(x_ref, mod_ref, ng_ref, win_hbm, wout_hbm, caw_ref, cab_ref, wg_ref, ba_ref, bi_ref,
                   lam_ref, cbw_ref, cbb_ref, lnbg_ref, lnbb_ref, lncg_ref, lncb_ref, fg_ref, ws_ref, bs_ref,
                   xo_ref, hista_ref, hlast_ref, histb_ref,
                   win_ref, wout_ref, stage, sem, xa_buf, glu_buf, cs_buf, hc_ref, y_buf, *, l, tl, final):
    c = pl.program_id(1)

    @pl.when((pl.program_id(0) == 0) & (c == 0))
    def _():
        _load_layer_weights(l, win_hbm, wout_hbm, win_ref, wout_ref, stage, sem)

    @pl.when(c == 0)
    def _():
        xa_buf[0:PAD_A, :] = jnp.zeros((PAD_A, W_A), F32)
        glu_buf[0:PAD_B, :] = jnp.zeros((PAD_B, W_B), F32)
        hc_ref[...] = jnp.zeros_like(hc_ref)

    x = x_ref[0]
    shift = mod_ref[0, :, 0:D_MODEL]
    scale = mod_ref[0, :, D_MODEL:2 * D_MODEL]
    gate = mod_ref[0, :, 2 * D_MODEL:3 * D_MODEL]
    xnb = _norm_mod(x, ng_ref[...], scale, shift).astype(BF16)

    xa_buf[PAD_A:PAD_A + tl, :] = _dot(xnb, _win(win_ref, O_XA, W_A))
    xac = cab_ref[...]
    for k in range(CONV_A):
        o = PAD_A - HIST_A + k
        xac = xac + caw_ref[k:k + 1, :] * xa_buf[o:o + tl, :]
    new_hist_a = xa_buf[PAD_A + tl - HIST_A:PAD_A + tl, :]
    hista_ref[0] = new_hist_a
    xa_buf[PAD_A - HIST_A:PAD_A, :] = new_hist_a

    a, b = _lru_coeffs(xac, wg_ref, ba_ref[...], bi_ref[...], lam_ref[...])
    ng = tl // SUBLANES
    a3, b3 = _group_scan(a.reshape(ng, SUBLANES, W_A), b.reshape(ng, SUBLANES, W_A))
    carry = hc_ref[...]
    hs = []
    for g in range(ng):
        hg = b3[g] + a3[g] * carry
        carry = hg[SUBLANES - 1:SUBLANES, :]
        hs.append(hg)
    h = jnp.concatenate(hs, axis=0)
    hc_ref[...] = carry
    hlast_ref[0] = carry
    ga = _dot(xnb, _win(win_ref, O_GA, W_A))
    y_buf[:, 0:W_A] = (h * _silu(ga)).astype(BF16)

    gla = _dot(xnb, _win(win_ref, O_GLA, W_B))
    glb = _dot(xnb, _win(win_ref, O_GLB, W_B))
    glu_buf[PAD_B:PAD_B + tl, :] = gla * _sigmoid(glb)
    ext = tl + PAD_B - SUBLANES
    for r in range(1, SUBLANES):
        cs_buf[r - 1] = glu_buf[r:r + ext, :]
    gb = _dot(xnb, _win(win_ref, O_GB, W_B))
    for rb in range(tl // ROW_BLK):
        r0 = rb * ROW_BLK
        cb = cbb_ref[...]
        for k in range(CONV_B):
            q, r = divmod(k + PAD_B - HIST_B, SUBLANES)
            o = r0 + q * SUBLANES
            src = glu_buf[o:o + ROW_BLK, :] if r == 0 else cs_buf[r - 1, o:o + ROW_BLK, :]
            cb = cb + cbw_ref[k:k + 1, :] * src
        yb = _silu(_layernorm(cb, lnbg_ref[...], lnbb_ref[...])) * _silu(gb[r0:r0 + ROW_BLK, :])
        y_buf[r0:r0 + ROW_BLK, W_A:W_A + W_B] = yb.astype(BF16)
    new_hist_b = glu_buf[PAD_B + tl - HIST_B:PAD_B + tl, :]
    histb_ref[0] = new_hist_b
    glu_buf[PAD_B - HIST_B:PAD_B, :] = new_hist_b

    v = _dot(xnb, _win(win_ref, O_V, W_C))
    vnb = _layernorm(v, lncg_ref[...], lncb_ref[...]).astype(BF16)
    tri = _tril_mask(CHUNK)
    lane = lax.broadcasted_iota(jnp.int32, (CHUNK, LANES), 1)
    lo_half = lane < HD_C
    wpair = []
    for p in range(H_C // 2):
        w0 = jnp.where(tri, ws_ref[2 * p], 0.0).astype(BF16)
        w1 = jnp.where(tri, ws_ref[2 * p + 1], 0.0).astype(BF16)
        wpair.append(jnp.concatenate([w0, w1], axis=1))
    zero = jnp.zeros((CHUNK, LANES), BF16)
    chunks = []
    for cc in range(tl // CHUNK):
        cols = []
        for p in range(H_C // 2):
            vp = vnb[cc * CHUNK:(cc + 1) * CHUNK, p * LANES:(p + 1) * LANES]
            rhs = jnp.concatenate([jnp.where(lo_half, vp, zero), jnp.where(lo_half, zero, vp)], axis=0)
            cols.append(_dot(wpair[p], rhs))
        chunks.append(jnp.concatenate(cols, axis=1) + bs_ref[...])
    mixed = jnp.concatenate(chunks, axis=0)
    u = _dot(xnb, _win(win_ref, O_U, W_C))
    gc = _dot(xnb, _win(win_ref, O_GC, W_C))
    y_buf[:, W_A + W_B:MIX] = (u * mixed * _silu(gc)).astype(BF16)

    out = _dot(y_buf[...], wout_ref[...])
    xnew = x + gate * out
    if final:
        xnew = _rmsnorm(xnew, fg_ref[...])
    xo_ref[0] = xnew


def _layer_spec(l, shape):
    nd = len(shape)
    return pl.BlockSpec((None,) + tuple(shape), lambda *_: (l,) + (0,) * nd, pipeline_mode=pl.Buffered(1))


def _const_spec(shape):
    nd = len(shape)
    return pl.BlockSpec(tuple(shape), lambda *_: (0,) * nd, pipeline_mode=pl.Buffered(1))


def _weight_scratch():
    return [
        pltpu.VMEM((D_MODEL, IN_COLS), BF16),
        pltpu.VMEM((MIX, D_MODEL), BF16),
        pltpu.VMEM((2, SLAB_ROWS, IN_COLS), F32),
        pltpu.SemaphoreType.DMA((2,)),
    ]


def _layer_param_specs(l):
    return [
        _layer_spec(l, (1, D_MODEL)),
        pl.BlockSpec(memory_space=pl.ANY),
        pl.BlockSpec(memory_space=pl.ANY),
        _layer_spec(l, (CONV_A, W_A)),
        _layer_spec(l, (1, W_A)),
        _layer_spec(l, (N_GATE_BLK, GATE_BLK, 2 * GATE_BLK)),
        _layer_spec(l, (1, W_A)),
        _layer_spec(l, (1, W_A)),
        _layer_spec(l, (1, W_A)),
        _layer_spec(l, (CONV_B, W_B)),
        _layer_spec(l, (1, W_B)),
        _layer_spec(l, (1, W_B)),
        _layer_spec(l, (1, W_B)),
        _layer_spec(l, (1, W_C)),
        _layer_spec(l, (1, W_C)),
        _const_spec((1, D_MODEL)),
    ]


def _prompt_layer(l, x, mod_p, p, tl=256):
    bp, seq, _ = x.shape
    kern = functools.partial(_prompt_kernel, l=l, tl=tl, final=(l == DEPTH - 1))
    in_specs = [
        pl.BlockSpec((1, tl, D_MODEL), lambda b, c: (b, c, 0)),
        pl.BlockSpec((None, 1, 1, 3 * D_MODEL), lambda b, c: (l, b, 0, 0)),
    ] + _layer_param_specs(l) + [
        _layer_spec(l, (H_C, CHUNK, CHUNK)),
        _layer_spec(l, (CHUNK, W_C)),
    ]
    out_specs = [
        pl.BlockSpec((1, tl, D_MODEL), lambda b, c: (b, c, 0)),
        pl.BlockSpec((1, HIST_A, W_A), lambda b, c: (b, 0, 0)),
        pl.BlockSpec((1, 1, W_A), lambda b, c: (b, 0, 0)),
        pl.BlockSpec((1, HIST_B, W_B), lambda b, c: (b, 0, 0)),
    ]
    out_shape = [
        jax.ShapeDtypeStruct((bp, seq, D_MODEL), F32),
        jax.ShapeDtypeStruct((bp, HIST_A, W_A), F32),
        jax.ShapeDtypeStruct((bp, 1, W_A), F32),
        jax.ShapeDtypeStruct((bp, HIST_B, W_B), F32),
    ]
    return pl.pallas_call(
        kern,
        grid=(bp, seq // tl),
        in_specs=in_specs,
        out_specs=out_specs,
        out_shape=out_shape,
        scratch_shapes=_weight_scratch() + [
            pltpu.VMEM((PAD_A + tl, W_A), F32),
            pltpu.VMEM((PAD_B + tl, W_B), F32),
            pltpu.VMEM((SUBLANES - 1, tl + PAD_B - SUBLANES, W_B), F32),
            pltpu.VMEM((1, W_A), F32),
            pltpu.VMEM((tl, MIX), BF16),
        ],
        compiler_params=pltpu.CompilerParams(
            dimension_semantics=("arbitrary", "arbitrary"), vmem_limit_bytes=VMEM_LIMIT),
        name="prompt_layer",
    )(x, mod_p.reshape(DEPTH, bp, 1, 3 * D_MODEL), *p["common"], p["ws"], p["bs_full"])


def _rows_to_tokens(ref, nb, lo, hi):
    return jnp.concatenate(
        [jnp.broadcast_to(ref[i:i + 1, lo:hi], (SUBLANES, hi - lo)) for i in range(nb)], axis=0)


def _sample_kernel(x_ref, mod_ref, ng_ref, win_hbm, wout_hbm, caw_ref, cab_ref, wg_ref, ba_ref, bi_ref,
                   lam_ref, cbw_ref, cbb_ref, lnbg_ref, lnbb_ref, lncg_ref, lncb_ref, fg_ref,
                   wtb_ref, wcol_ref, bs8_ref, hista_in, h0_in, histb_in,
                   xo_ref, hista_out, hlast_out, histb_out, vn_out,
                   win_ref, wout_ref, stage, sem, xa_buf, y_buf, *, l, nb, final):
    del cbw_ref

    @pl.when(pl.program_id(0) == 0)
    def _():
        _load_layer_weights(l, win_hbm, wout_hbm, win_ref, wout_ref, stage, sem)

    t = nb * SUBLANES
    x = x_ref[...].reshape(t, D_MODEL)
    shift = _rows_to_tokens(mod_ref, nb, 0, D_MODEL)
    scale = _rows_to_tokens(mod_ref, nb, D_MODEL, 2 * D_MODEL)
    xnb = _norm_mod(x, ng_ref[...], scale, shift).astype(BF16)

    xa_buf[:, PAD_A:PAD_A + SUBLANES, :] = _dot(xnb, _win(win_ref, O_XA, W_A)).reshape(nb, SUBLANES, W_A)
    xa_buf[:, PAD_A - HIST_A:PAD_A, :] = hista_in[...]
    xac3 = cab_ref[...]
    for k in range(CONV_A):
        o = PAD_A - HIST_A + k
        xac3 = xac3 + caw_ref[k:k + 1, :] * xa_buf[:, o:o + SUBLANES, :]
    hista_out[...] = xa_buf[:, PAD_A + SUBLANES - HIST_A:PAD_A + SUBLANES, :]
    a, b = _lru_coeffs(xac3.reshape(t, W_A), wg_ref, ba_ref[...], bi_ref[...], lam_ref[...])
    a3, b3 = _group_scan(a.reshape(nb, SUBLANES, W_A), b.reshape(nb, SUBLANES, W_A))
    h3 = b3 + a3 * h0_in[...]
    hlast_out[...] = h3[:, SUBLANES - 1:SUBLANES, :]
    ga = _dot(xnb, _win(win_ref, O_GA, W_A))
    y_buf[:, 0:W_A] = (h3.reshape(t, W_A) * _silu(ga)).astype(BF16)

    gla = _dot(xnb, _win(win_ref, O_GLA, W_B))
    glb = _dot(xnb, _win(win_ref, O_GLB, W_B))
    glu3 = (gla * _sigmoid(glb)).reshape(nb, SUBLANES, W_B)
    gb = _dot(xnb, _win(win_ref, O_GB, W_B))
    for sb in range(nb // SEQ_BLK):
        s0 = sb * SEQ_BLK
        cb3 = cbb_ref[...]
        for j in range(HIST_B + SUBLANES):
            if j < HIST_B:
                row = histb_in[s0:s0 + SEQ_BLK, j:j + 1, :]
            else:
                row = glu3[s0:s0 + SEQ_BLK, j - HIST_B:j - HIST_B + 1, :]
            cb3 = cb3 + wtb_ref[j] * row
        cb = cb3.reshape(SEQ_BLK * SUBLANES, W_B)
        r0 = s0 * SUBLANES
        yb = _silu(_layernorm(cb, lnbg_ref[...], lnbb_ref[...])) * _silu(gb[r0:r0 + SEQ_BLK * SUBLANES, :])
        y_buf[r0:r0 + SEQ_BLK * SUBLANES, W_A:W_A + W_B] = yb.astype(BF16)
    histb_out[:, 0:HIST_B - SUBLANES, :] = histb_in[:, SUBLANES:HIST_B, :]
    histb_out[:, HIST_B - SUBLANES:HIST_B, :] = glu3

    v = _dot(xnb, _win(win_ref, O_V, W_C))
    vn3 = _layernorm(v, lncg_ref[...], lncb_ref[...]).reshape(nb, SUBLANES, W_C)
    vn_out[...] = vn3
    trow = lax.broadcasted_iota(jnp.int32, (SUBLANES, W_C), 0)
    mixed3 = jnp.broadcast_to(bs8_ref[...], (nb, SUBLANES, W_C))
    for s in range(SUBLANES):
        wc = jnp.where(trow >= s, wcol_ref[s], 0.0)
        mixed3 = mixed3 + wc * vn3[:, s:s + 1, :]
    u = _dot(xnb, _win(win_ref, O_U, W_C))
    gc = _dot(xnb, _win(win_ref, O_GC, W_C))
    y_buf[:, W_A + W_B:MIX] = (u * mixed3.reshape(t, W_C) * _silu(gc)).astype(BF16)

    out = _dot(y_buf[...], wout_ref[...])
    gate = _rows_to_tokens(mod_ref, nb, 2 * D_MODEL, 3 * D_MODEL)
    xnew = x + gate * out
    if final:
        xnew = _rmsnorm(xnew, fg_ref[...])
    xo_ref[...] = xnew.reshape(nb, SUBLANES, D_MODEL)


def _sample_layer(l, x, mod_s, hist_a, h0, hist_b, p, nb=16):
    bs, ls, _ = x.shape
    assert ls == SUBLANES
    kern = functools.partial(_sample_kernel, l=l, nb=nb, final=(l == DEPTH - 1))
    seq_spec = lambda rows, w: pl.BlockSpec((nb, rows, w), lambda i: (i, 0, 0))
    state_spec = lambda rows, w: pl.BlockSpec((None, nb, rows, w), lambda i: (l, i, 0, 0))
    in_specs = [
        seq_spec(SUBLANES, D_MODEL),
        pl.BlockSpec((None, nb, 3 * D_MODEL), lambda i: (l, i, 0)),
    ] + _layer_param_specs(l) + [
        _layer_spec(l, (HIST_B + SUBLANES, SUBLANES, W_B)),
        _layer_spec(l, (SUBLANES, SUBLANES, W_C)),
        _layer_spec(l, (SUBLANES, W_C)),
        state_spec(HIST_A, W_A),
        state_spec(1, W_A),
        state_spec(HIST_B, W_B),
    ]
    out_specs = [
        seq_spec(SUBLANES, D_MODEL),
        seq_spec(HIST_A, W_A),
        seq_spec(1, W_A),
        seq_spec(HIST_B, W_B),
        seq_spec(SUBLANES, W_C),
    ]
    out_shape = [
        jax.ShapeDtypeStruct((bs, SUBLANES, D_MODEL), F32),
        jax.ShapeDtypeStruct((bs, HIST_A, W_A), F32),
        jax.ShapeDtypeStruct((bs, 1, W_A), F32),
        jax.ShapeDtypeStruct((bs, HIST_B, W_B), F32),
        jax.ShapeDtypeStruct((bs, SUBLANES, W_C), F32),
    ]
    return pl.pallas_call(
        kern,
        grid=(bs // nb,),
        in_specs=in_specs,
        out_specs=out_specs,
        out_shape=out_shape,
        scratch_shapes=_weight_scratch() + [
            pltpu.VMEM((nb, PAD_A + SUBLANES, W_A), F32),
            pltpu.VMEM((nb * SUBLANES, MIX), BF16),
        ],
        compiler_params=pltpu.CompilerParams(
            dimension_semantics=("arbitrary",), vmem_limit_bytes=VMEM_LIMIT),
        name="sample_layer",
    )(x, mod_s, *p["common"], p["wtb"], p["wcol"], p["bs8"], hist_a, h0, hist_b)


def _block_diag_gates(wa, wi):
    per = GATE_BLK // HD_A
    eye = jnp.eye(per, dtype=wa.dtype)

    def bd(w):
        w5 = w.reshape(DEPTH, N_GATE_BLK, per, HD_A, HD_A)
        return jnp.einsum("ljpde,pq->ljpdqe", w5, eye).reshape(DEPTH, N_GATE_BLK, GATE_BLK, GATE_BLK)

    return jnp.concatenate([bd(wa), bd(wi)], axis=-1)


def _toeplitz_tiles(conv_b_w):
    j = jnp.arange(HIST_B + SUBLANES)[:, None]
    t = jnp.arange(SUBLANES)[None, :]
    idx = j - t
    valid = (idx >= 0) & (idx < CONV_B)
    tiles = conv_b_w[:, jnp.clip(idx, 0, CONV_B - 1), :]
    return jnp.where(valid[None, :, :, None], tiles, 0.0)


def _prep_params(norm_g, w_in, conv_a_w, conv_a_b, lru_wa, lru_ba, lru_wi, lru_bi, lru_lam, conv_b_w,
                 conv_b_b, ln_b_g, ln_b_b, ln_c_g, ln_c_b, gmlp_ws, gmlp_bs, w_out, final_g):
    row = lambda v: v.reshape(DEPTH, 1, -1)
    common = (
        row(norm_g), w_in, w_out, conv_a_w, row(conv_a_b),
        _block_diag_gates(lru_wa, lru_wi).astype(BF16), row(lru_ba), row(lru_bi), row(lru_lam),
        conv_b_w, row(conv_b_b), row(ln_b_g), row(ln_b_b), row(ln_c_g), row(ln_c_b),
        final_g.reshape(1, D_MODEL),
    )
    bs_full = jnp.repeat(jnp.transpose(gmlp_bs, (0, 2, 1)), HD_C, axis=2)
    wcol = jnp.repeat(jnp.transpose(gmlp_ws[:, :, :SUBLANES, :SUBLANES], (0, 3, 2, 1)), HD_C, axis=3)
    return {"common": common, "ws": gmlp_ws, "bs_full": bs_full, "wcol": wcol,
            "bs8": bs_full[:, :SUBLANES], "wtb": _toeplitz_tiles(conv_b_w)}


def kernel(x_prompt, x_sample, c_prompt, c_sample, state_lru_conv, state_lru_h, state_ccm_conv, norm_g, w_ada,
           b_ada, w_in, conv_a_w, conv_a_b, lru_wa, lru_ba, lru_wi, lru_bi, lru_lam, conv_b_w, conv_b_b, ln_b_g,
           ln_b_b, ln_c_g, ln_c_b, gmlp_ws, gmlp_bs, w_out, final_g):
    bp = x_prompt.shape[0]
    bs = x_sample.shape[0]
    mod_p, mod_s = _ada_call(c_prompt, c_sample, w_ada, b_ada)
    p = _prep_params(norm_g, w_in, conv_a_w, conv_a_b, lru_wa, lru_ba, lru_wi, lru_bi, lru_lam, conv_b_w,
                     conv_b_b, ln_b_g, ln_b_b, ln_c_g, ln_c_b, gmlp_ws, gmlp_bs, w_out, final_g)
    h0_all = state_lru_h.reshape(DEPTH, bs, 1, W_A)
    xp, xs = x_prompt, x_sample
    conv_a_p, h_p, conv_b_p = [], [], []
    conv_a_s, h_s, conv_b_s, v_s = [], [], [], []
    for l in range(DEPTH):
        xp, ha, hl, hb = _prompt_layer(l, xp, mod_p, p)
        conv_a_p.append(ha); h_p.append(hl.reshape(bp, W_A)); conv_b_p.append(hb)
        xs, ha, hl, hb, vn = _sample_layer(l, xs, mod_s, state_lru_conv, h0_all, state_ccm_conv, p)
        conv_a_s.append(ha); h_s.append(hl.reshape(bs, W_A)); conv_b_s.append(hb); v_s.append(vn)
    return (xp, xs,
            jnp.stack(conv_a_p), jnp.stack(h_p), jnp.stack(conv_b_p),
            jnp.stack(conv_a_s), jnp.stack(h_s), jnp.stack(conv_b_s), jnp.stack(v_s))
```

```python
import functools

import jax
import jax.numpy as jnp
from jax import lax
from jax.experimental import pallas as pl
from jax.experimental.pallas import tpu as pltpu

F32 = jnp.float32
BF16 = jnp.bfloat16

D_MODEL = 2048
DEPTH = 4
MIX = D_MODEL
W_A = MIX // 2
H_A = 16
HD_A = W_A // H_A
CONV_A = 4
C_LRU = 8.0
W_B = MIX // 4
CONV_B = 31
W_C = MIX // 4
H_C = 8
HD_C = W_C // H_C
CHUNK = 128
EPS = 1e-6
IN_COLS = 2 * W_A + 3 * W_B + 3 * W_C

O_XA = 0
O_GA = O_XA + W_A
O_GLA = O_GA + W_A
O_GLB = O_GLA + W_B
O_GB = O_GLB + W_B
O_U = O_GB + W_B
O_V = O_U + W_C
O_GC = O_V + W_C

SUBLANES = 8
LANES = 128
GATE_BLK = 256
N_GATE_BLK = W_A // GATE_BLK
HIST_A = CONV_A - 1
HIST_B = CONV_B - 1
PAD_A = SUBLANES
PAD_B = 32
SLAB_ROWS = 64
N_STAGE = 4
ROW_BLK = 64
SEQ_BLK = 4

VMEM_LIMIT = 58 * 1024 * 1024


def _dot(a, b):
    return jnp.dot(a, b, preferred_element_type=F32)


def _slab_copy(w_hbm, l, s, width, stage, sem):
    slot = s % N_STAGE
    return pltpu.make_async_copy(
        w_hbm.at[l, pl.ds(s * SLAB_ROWS, SLAB_ROWS), :], stage.at[slot, :, 0:width], sem.at[slot])


def _load_cast_weights(w_hbm, l, dst, stage, sem):
    rows, width = dst.shape
    n_slabs = rows // SLAB_ROWS
    ahead = N_STAGE - 1
    for s in range(ahead):
        _slab_copy(w_hbm, l, s, width, stage, sem).start()

    def body(s, carry):
        @pl.when(s + ahead < n_slabs)
        def _():
            _slab_copy(w_hbm, l, s + ahead, width, stage, sem).start()

        _slab_copy(w_hbm, l, s, width, stage, sem).wait()
        r0 = pl.multiple_of(s * SLAB_ROWS, SLAB_ROWS)
        dst[pl.ds(r0, SLAB_ROWS), :] = stage[s % N_STAGE, :, 0:width].astype(BF16)
        return carry

    lax.fori_loop(0, n_slabs, body, 0)


def _load_layer_weights(l, win_hbm, wout_hbm, win_s, wout_s, stage, sem):
    _load_cast_weights(win_hbm, l, win_s, stage, sem)
    _load_cast_weights(wout_hbm, l, wout_s, stage, sem)


def _sigmoid(x):
    return 0.5 * jnp.tanh(0.5 * x) + 0.5


def _silu(x):
    return x * _sigmoid(x)


def _layernorm(x, g, b):
    mu = jnp.mean(x, axis=-1, keepdims=True)
    xc = x - mu
    return xc * lax.rsqrt(jnp.mean(xc * xc, axis=-1, keepdims=True) + EPS) * g + b


def _rmsnorm(x, g):
    return (x * lax.rsqrt(jnp.mean(x * x, axis=-1, keepdims=True) + EPS)) * g


def _norm_mod(x, g, scale, shift):
    return _rmsnorm(x, g) * (1.0 + scale) + shift


def _lru_coeffs(xac, wg_ref, ba, bi, lam):
    xacb = xac.astype(BF16)
    pa, pi = [], []
    for j in range(N_GATE_BLK):
        g = _dot(xacb[:, j * GATE_BLK:(j + 1) * GATE_BLK], wg_ref[j])
        pa.append(g[:, :GATE_BLK])
        pi.append(g[:, GATE_BLK:])
    r = _sigmoid(jnp.concatenate(pa, axis=1) + ba)
    i = _sigmoid(jnp.concatenate(pi, axis=1) + bi)
    log_a = (-C_LRU * jax.nn.softplus(-lam)) * r
    a = jnp.exp(log_a)
    coef = jnp.sqrt(-jnp.tanh(log_a) * (1.0 + a * a))
    return a, coef * (i * xac)


def _group_scan(a3, b3):
    row = lax.broadcasted_iota(jnp.int32, (1,) + a3.shape[1:], 1)
    for s in (1, 2, 4):
        keep = row >= s
        a_prev = jnp.where(keep, pltpu.roll(a3, s, axis=1), 1.0)
        b_prev = jnp.where(keep, pltpu.roll(b3, s, axis=1), 0.0)
        b3 = a3 * b_prev + b3
        a3 = a3 * a_prev
    return a3, b3


def _tril_mask(n):
    r = lax.broadcasted_iota(jnp.int32, (n, n), 0)
    c = lax.broadcasted_iota(jnp.int32, (n, n), 1)
    return c <= r


def _win(win_ref, off, width):
    return win_ref[:, off:off + width]


def _ada_kernel(cp_ref, cs_ref, w_ref, b_ref, mp_ref, ms_ref):
    w = w_ref[0].astype(BF16)
    b = b_ref[0]
    mp_ref[0] = _dot(_silu(cp_ref[...]).astype(BF16), w) + b
    ms_ref[0] = _dot(_silu(cs_ref[...]).astype(BF16), w) + b


def _ada_call(c_prompt, c_sample, w_ada, b_ada, tn=1024):
    bp, bs = c_prompt.shape[0], c_sample.shape[0]
    n3 = w_ada.shape[-1]
    return pl.pallas_call(
        _ada_kernel,
        grid=(DEPTH, n3 // tn),
        in_specs=[
            pl.BlockSpec((bp, D_MODEL), lambda l, j: (0, 0)),
            pl.BlockSpec((bs, D_MODEL), lambda l, j: (0, 0)),
            pl.BlockSpec((1, D_MODEL, tn), lambda l, j: (l, 0, j)),
            pl.BlockSpec((1, 1, tn), lambda l, j: (l, 0, j)),
        ],
        out_specs=[
            pl.BlockSpec((1, bp, tn), lambda l, j: (l, 0, j)),
            pl.BlockSpec((1, bs, tn), lambda l, j: (l, 0, j)),
        ],
        out_shape=[
            jax.ShapeDtypeStruct((DEPTH, bp, n3), F32),
            jax.ShapeDtypeStruct((DEPTH, bs, n3), F32),
        ],
        compiler_params=pltpu.CompilerParams(
            dimension_semantics=("arbitrary", "arbitrary"), vmem_limit_bytes=VMEM_LIMIT),
        name="ada_mod",
    )(c_prompt, c_sample, w_ada, b_ada.reshape(DEPTH, 1, n3))


def _prompt_kernel(x_ref, mod_ref, ng_ref, win_hbm, wout_hbm, caw_ref, cab_ref, wg_ref, ba_ref, bi_ref,
                   lam_ref, cbw_ref, cbb_ref, lnbg_ref, lnbb_ref, lncg_ref, lncb_ref, fg_ref, ws_ref, bs_ref,
                   xo_ref, hista_ref, hlast_ref, histb_ref,
                   win_ref, wout_ref, stage, sem, xa_buf, glu_buf, cs_buf, hc_ref, y_buf, *, l, tl, final):
    c = pl.program_id(1)

    @pl.when((pl.program_id(0) == 0) & (c == 0))
    def _():
        _load_layer_weights(l, win_hbm, wout_hbm, win_ref, wout_ref, stage, sem)

    @pl.when(c == 0)
    def _():
        xa_buf[0:PAD_A, :] = jnp.zeros((PAD_A, W_A), F32)
        glu_buf[0:PAD_B, :] = jnp.zeros((PAD_B, W_B), F32)
        hc_ref[...] = jnp.zeros_like(hc_ref)

    x = x_ref[0]
    shift = mod_ref[0, :, 0:D_MODEL]
    scale = mod_ref[0, :, D_MODEL:2 * D_MODEL]
    gate = mod_ref[0, :, 2 * D_MODEL:3 * D_MODEL]
    xnb = _norm_mod(x, ng_ref[...], scale, shift).astype(BF16)

    xa_buf[PAD_A:PAD_A + tl, :] = _dot(xnb, _win(win_ref, O_XA, W_A))
    xac = cab_ref[...]
    for k in range(CONV_A):
        o = PAD_A - HIST_A + k
        xac = xac + caw_ref[k:k + 1, :] * xa_buf[o:o + tl, :]
    new_hist_a = xa_buf[PAD_A + tl - HIST_A:PAD_A + tl, :]
    hista_ref[0] = new_hist_a
    xa_buf[PAD_A - HIST_A:PAD_A, :] = new_hist_a

    a, b = _lru_coeffs(xac, wg_ref, ba_ref[...], bi_ref[...], lam_ref[...])
    ng = tl // SUBLANES
    a3, b3 = _group_scan(a.reshape(ng, SUBLANES, W_A), b.reshape(ng, SUBLANES, W_A))
    carry = hc_ref[...]
    hs = []
    for g in range(ng):
        hg = b3[g] + a3[g] * carry
        carry = hg[SUBLANES - 1:SUBLANES, :]
        hs.append(hg)
    h = jnp.concatenate(hs, axis=0)
    hc_ref[...] = carry
    hlast_ref[0] = carry
    ga = _dot(xnb, _win(win_ref, O_GA, W_A))
    y_buf[:, 0:W_A] = (h * _silu(ga)).astype(BF16)

    gla = _dot(xnb, _win(win_ref, O_GLA, W_B))
    glb = _dot(xnb, _win(win_ref, O_GLB, W_B))
    glu_buf[PAD_B:PAD_B + tl, :] = gla * _sigmoid(glb)
    ext = tl + PAD_B - SUBLANES
    for r in range(1, SUBLANES):
        cs_buf[r - 1] = glu_buf[r:r + ext, :]
    gb = _dot(xnb, _win(win_ref, O_GB, W_B))
    for rb in range(tl // ROW_BLK):
        r0 = rb * ROW_BLK
        cb = cbb_ref[...]
        for k in range(CONV_B):
            q, r = divmod(k + PAD_B - HIST_B, SUBLANES)
            o = r0 + q * SUBLANES
            src = glu_buf[o:o + ROW_BLK, :] if r == 0 else cs_buf[r - 1, o:o + ROW_BLK, :]
            cb = cb + cbw_ref[k:k + 1, :] * src
        yb = _silu(_layernorm(cb, lnbg_ref[...], lnbb_ref[...])) * _silu(gb[r0:r0 + ROW_BLK, :])
        y_buf[r0:r0 + ROW_BLK, W_A:W_A + W_B] = yb.astype(BF16)
    new_hist_b = glu_buf[PAD_B + tl - HIST_B:PAD_B + tl, :]
    histb_ref[0] = new_hist_b
    glu_buf[PAD_B - HIST_B:PAD_B, :] = new_hist_b

    v = _dot(xnb, _win(win_ref, O_V, W_C))
    vnb = _layernorm(v, lncg_ref[...], lncb_ref[...]).astype(BF16)
    tri = _tril_mask(CHUNK)
    lane = lax.broadcasted_iota(jnp.int32, (CHUNK, LANES), 1)
    lo_half = lane < HD_C
    wpair = []
    for p in range(H_C // 2):
        w0 = jnp.where(tri, ws_ref[2 * p], 0.0).astype(BF16)
        w1 = jnp.where(tri, ws_ref[2 * p + 1], 0.0).astype(BF16)
        wpair.append(jnp.concatenate([w0, w1], axis=1))
    zero = jnp.zeros((CHUNK, LANES), BF16)
    chunks = []
    for cc in range(tl // CHUNK):
        cols = []
        for p in range(H_C // 2):
            vp = vnb[cc * CHUNK:(cc + 1) * CHUNK, p * LANES:(p + 1) * LANES]
            rhs = jnp.concatenate([jnp.where(lo_half, vp, zero), jnp.where(lo_half, zero, vp)], axis=0)
            cols.append(_dot(wpair[p], rhs))
        chunks.append(jnp.concatenate(cols, axis=1) + bs_ref[...])
    mixed = jnp.concatenate(chunks, axis=0)
    u = _dot(xnb, _win(win_ref, O_U, W_C))
    gc = _dot(xnb, _win(win_ref, O_GC, W_C))
    y_buf[:, W_A + W_B:MIX] = (u * mixed * _silu(gc)).astype(BF16)

    out = _dot(y_buf[...], wout_ref[...])
    xnew = x + gate * out
    if final:
        xnew = _rmsnorm(xnew, fg_ref[...])
    xo_ref[0] = xnew


def _layer_spec(l, shape):
    nd = len(shape)
    return pl.BlockSpec((None,) + tuple(shape), lambda *_: (l,) + (0,) * nd, pipeline_mode=pl.Buffered(1))


def _const_spec(shape):
    nd = len(shape)
    return pl.BlockSpec(tuple(shape), lambda *_: (0,) * nd, pipeline_mode=pl.Buffered(1))


def _weight_scratch():
    return [
        pltpu.VMEM((D_MODEL, IN_COLS), BF16),
        pltpu.VMEM((MIX, D_MODEL), BF16),
        pltpu.VMEM((N_STAGE, SLAB_ROWS, IN_COLS), F32),
        pltpu.SemaphoreType.DMA((N_STAGE,)),
    ]


def _layer_param_specs(l):
    return [
        _layer_spec(l, (1, D_MODEL)),
        pl.BlockSpec(memory_space=pl.ANY),
        pl.BlockSpec(memory_space=pl.ANY),
        _layer_spec(l, (CONV_A, W_A)),
        _layer_spec(l, (1, W_A)),
        _layer_spec(l, (N_GATE_BLK, GATE_BLK, 2 * GATE_BLK)),
        _layer_spec(l, (1, W_A)),
        _layer_spec(l, (1, W_A)),
        _layer_spec(l, (1, W_A)),
        _layer_spec(l, (CONV_B, W_B)),
        _layer_spec(l, (1, W_B)),
        _layer_spec(l, (1, W_B)),
        _layer_spec(l, (1, W_B)),
        _layer_spec(l, (1, W_C)),
        _layer_spec(l, (1, W_C)),
        _const_spec((1, D_MODEL)),
    ]


def _prompt_layer(l, x, mod_p, p, tl=256):
    bp, seq, _ = x.shape
    kern = functools.partial(_prompt_kernel, l=l, tl=tl, final=(l == DEPTH - 1))
    in_specs = [
        pl.BlockSpec((1, tl, D_MODEL), lambda b, c: (b, c, 0)),
        pl.BlockSpec((None, 1, 1, 3 * D_MODEL), lambda b, c: (l, b, 0, 0)),
    ] + _layer_param_specs(l) + [
        _layer_spec(l, (H_C, CHUNK, CHUNK)),
        _layer_spec(l, (CHUNK, W_C)),
    ]
    out_specs = [
        pl.BlockSpec((1, tl, D_MODEL), lambda b, c: (b, c, 0)),
        pl.BlockSpec((1, HIST_A, W_A), lambda b, c: (b, 0, 0)),
        pl.BlockSpec((1, 1, W_A), lambda b, c: (b, 0, 0)),
        pl.BlockSpec((1, HIST_B, W_B), lambda b, c: (b, 0, 0)),
    ]
    out_shape = [
        jax.ShapeDtypeStruct((bp, seq, D_MODEL), F32),
        jax.ShapeDtypeStruct((bp, HIST_A, W_A), F32),
        jax.ShapeDtypeStruct((bp, 1, W_A), F32),
        jax.ShapeDtypeStruct((bp, HIST_B, W_B), F32),
    ]
    return pl.pallas_call(
        kern,
        grid=(bp, seq // tl),
        in_specs=in_specs,
        out_specs=out_specs,
        out_shape=out_shape,
        scratch_shapes=_weight_scratch() + [
            pltpu.VMEM((PAD_A + tl, W_A), F32),
            pltpu.VMEM((PAD_B + tl, W_B), F32),
            pltpu.VMEM((SUBLANES - 1, tl + PAD_B - SUBLANES, W_B), F32),
            pltpu.VMEM((1, W_A), F32),
            pltpu.VMEM((tl, MIX), BF16),
        ],
        compiler_params=pltpu.CompilerParams(
            dimension_semantics=("arbitrary", "arbitrary"), vmem_limit_bytes=VMEM_LIMIT),
        name="prompt_layer",
    )(x, mod_p.reshape(DEPTH, bp, 1, 3 * D_MODEL), *p["common"], p["ws"], p["bs_full"])


def _rows_to_groups(get_row, n, width):
    rows = [jnp.broadcast_to(get_row(i), (SUBLANES, width)) for i in range(n)]
    return jnp.concatenate(rows, axis=0).reshape(n, SUBLANES, width)


def _rows_to_tokens(ref, nb, lo, hi):
    return _rows_to_groups(lambda i: ref[i:i + 1, lo:hi], nb, hi - lo).reshape(nb * SUBLANES, hi - lo)


def _sample_kernel(*refs, l, nb, final, n_alias):
    (x_ref, mod_ref, ng_ref, win_hbm, wout_hbm, caw_ref, cab_ref, wg_ref, ba_ref, bi_ref,
     lam_ref, cbw_ref, cbb_ref, lnbg_ref, lnbb_ref, lncg_ref, lncb_ref, fg_ref,
     wta_ref, wtb_ref, wcol_ref, bs8_ref, hista_in, h0_in, histb_in) = refs[:25]
    (xo_ref, hista_out, hlast_out, histb_out, vn_out,
     win_ref, wout_ref, stage, sem, rows_buf, glu_rows, y_buf) = refs[25 + n_alias:]
    del cbw_ref

    @pl.when(pl.program_id(0) == 0)
    def _():
        _load_layer_weights(l, win_hbm, wout_hbm, win_ref, wout_ref, stage, sem)

    t = nb * SUBLANES
    x = x_ref[...].reshape(t, D_MODEL)
    shift = _rows_to_tokens(mod_ref, nb, 0, D_MODEL)
    scale = _rows_to_tokens(mod_ref, nb, D_MODEL, 2 * D_MODEL)
    xnb = _norm_mod(x, ng_ref[...], scale, shift).astype(BF16)

    xa3 = _dot(xnb, _win(win_ref, O_XA, W_A)).reshape(nb, SUBLANES, W_A)
    rows_buf[...] = xa3
    row = lax.broadcasted_iota(jnp.int32, (1, SUBLANES, W_A), 1)
    xac3 = cab_ref[...] + caw_ref[HIST_A:HIST_A + 1, :] * xa3
    for s in range(1, CONV_A):
        shifted = jnp.where(row >= s, pltpu.roll(xa3, s, axis=1), 0.0)
        xac3 = xac3 + caw_ref[HIST_A - s:HIST_A - s + 1, :] * shifted
    for j in range(HIST_A):
        hist3 = _rows_to_groups(lambda i, j=j: hista_in[j, i:i + 1, :], nb, W_A)
        xac3 = xac3 + wta_ref[j] * hist3
    for k in range(HIST_A):
        hista_out[k] = rows_buf[:, SUBLANES - HIST_A + k, :]
    a, b = _lru_coeffs(xac3.reshape(t, W_A), wg_ref, ba_ref[...], bi_ref[...], lam_ref[...])
    a3, b3 = _group_scan(a.reshape(nb, SUBLANES, W_A), b.reshape(nb, SUBLANES, W_A))
    h3 = b3 + a3 * _rows_to_groups(lambda i: h0_in[i:i + 1, :], nb, W_A)
    rows_buf[...] = h3
    hlast_out[...] = rows_buf[:, SUBLANES - 1, :]
    ga = _dot(xnb, _win(win_ref, O_GA, W_A))
    y_buf[:, 0:W_A] = (h3.reshape(t, W_A) * _silu(ga)).astype(BF16)

    gla = _dot(xnb, _win(win_ref, O_GLA, W_B))
    glb = _dot(xnb, _win(win_ref, O_GLB, W_B))
    glu3 = (gla * _sigmoid(glb)).reshape(nb, SUBLANES, W_B)
    glu_rows[...] = glu3
    gb = _dot(xnb, _win(win_ref, O_GB, W_B))
    for sb in range(nb // SEQ_BLK):
        s0 = sb * SEQ_BLK
        cb3 = cbb_ref[...]
        for j in range(HIST_B + SUBLANES):
            if j < HIST_B:
                rowj = _rows_to_groups(lambda i, j=j: histb_in[j, s0 + i:s0 + i + 1, :], SEQ_BLK, W_B)
            else:
                rowj = glu3[s0:s0 + SEQ_BLK, j - HIST_B:j - HIST_B + 1, :]
            cb3 = cb3 + wtb_ref[j] * rowj
        cb = cb3.reshape(SEQ_BLK * SUBLANES, W_B)
        r0 = s0 * SUBLANES
        yb = _silu(_layernorm(cb, lnbg_ref[...], lnbb_ref[...])) * _silu(gb[r0:r0 + SEQ_BLK * SUBLANES, :])
        y_buf[r0:r0 + SEQ_BLK * SUBLANES, W_A:W_A + W_B] = yb.astype(BF16)
    histb_out[0:HIST_B - SUBLANES] = histb_in[SUBLANES:HIST_B]
    for s in range(SUBLANES):
        histb_out[HIST_B - SUBLANES + s] = glu_rows[:, s, :]

    v = _dot(xnb, _win(win_ref, O_V, W_C))
    vn3 = _layernorm(v, lncg_ref[...], lncb_ref[...]).reshape(nb, SUBLANES, W_C)
    vn_out[...] = vn3
    trow = lax.broadcasted_iota(jnp.int32, (SUBLANES, W_C), 0)
    mixed3 = jnp.broadcast_to(bs8_ref[...], (nb, SUBLANES, W_C))
    for s in range(SUBLANES):
        wc = jnp.where(trow >= s, wcol_ref[s], 0.0)
        mixed3 = mixed3 + wc * vn3[:, s:s + 1, :]
    u = _dot(xnb, _win(win_ref, O_U, W_C))
    gc = _dot(xnb, _win(win_ref, O_GC, W_C))
    y_buf[:, W_A + W_B:MIX] = (u * mixed3.reshape(t, W_C) * _silu(gc)).astype(BF16)

    out = _dot(y_buf[...], wout_ref[...])
    gate = _rows_to_tokens(mod_ref, nb, 2 * D_MODEL, 3 * D_MODEL)
    xnew = x + gate * out
    if final:
        xnew = _rmsnorm(xnew, fg_ref[...])
    xo_ref[...] = xnew.reshape(nb, SUBLANES, D_MODEL)


def _sample_layer(l, x, mod_s, hist_a, h0, hist_b, p, prev, nb=16):
    bs, ls, _ = x.shape
    assert ls == SUBLANES
    out_shape = [
        jax.ShapeDtypeStruct((bs, SUBLANES, D_MODEL), F32),
        jax.ShapeDtypeStruct((DEPTH, HIST_A, bs, W_A), F32),
        jax.ShapeDtypeStruct((DEPTH, bs, W_A), F32),
        jax.ShapeDtypeStruct((DEPTH, HIST_B, bs, W_B), F32),
        jax.ShapeDtypeStruct((DEPTH, bs, SUBLANES, W_C), F32),
    ]
    if prev is None:
        prev = tuple(jnp.zeros(s.shape, s.dtype) for s in out_shape[1:])
    n_alias = len(prev)
    kern = functools.partial(_sample_kernel, l=l, nb=nb, final=(l == DEPTH - 1), n_alias=n_alias)
    in_specs = [
        pl.BlockSpec((nb, SUBLANES, D_MODEL), lambda i: (i, 0, 0)),
        pl.BlockSpec((None, nb, 3 * D_MODEL), lambda i: (l, i, 0)),
    ] + _layer_param_specs(l) + [
        _layer_spec(l, (HIST_A, SUBLANES, W_A)),
        _layer_spec(l, (HIST_B + SUBLANES, SUBLANES, W_B)),
        _layer_spec(l, (SUBLANES, SUBLANES, W_C)),
        _layer_spec(l, (SUBLANES, W_C)),
        pl.BlockSpec((None, HIST_A, nb, W_A), lambda i: (l, 0, i, 0)),
        pl.BlockSpec((None, nb, W_A), lambda i: (l, i, 0)),
        pl.BlockSpec((None, HIST_B, nb, W_B), lambda i: (l, 0, i, 0)),
    ] + [pl.BlockSpec(memory_space=pl.ANY)] * n_alias
    out_specs = [
        pl.BlockSpec((nb, SUBLANES, D_MODEL), lambda i: (i, 0, 0)),
        pl.BlockSpec((None, HIST_A, nb, W_A), lambda i: (l, 0, i, 0)),
        pl.BlockSpec((None, nb, W_A), lambda i: (l, i, 0)),
        pl.BlockSpec((None, HIST_B, nb, W_B), lambda i: (l, 0, i, 0)),
        pl.BlockSpec((None, nb, SUBLANES, W_C), lambda i: (l, i, 0, 0)),
    ]
    args = [x, mod_s, *p["common"], p["wta"], p["wtb"], p["wcol"], p["bs8"], hist_a, h0, hist_b]
    aliases = {len(args) + k: 1 + k for k in range(n_alias)}
    args += list(prev)
    outs = pl.pallas_call(
        kern,
        grid=(bs // nb,),
        in_specs=in_specs,
        out_specs=out_specs,
        out_shape=out_shape,
        input_output_aliases=aliases,
        scratch_shapes=_weight_scratch() + [
            pltpu.VMEM((nb, SUBLANES, W_A), F32),
            pltpu.VMEM((nb, SUBLANES, W_B), F32),
            pltpu.VMEM((nb * SUBLANES, MIX), BF16),
        ],
        compiler_params=pltpu.CompilerParams(
            dimension_semantics=("arbitrary",), vmem_limit_bytes=VMEM_LIMIT),
        name="sample_layer",
    )(*args)
    return outs[0], tuple(outs[1:])


def _block_diag_gates(wa, wi):
    per = GATE_BLK // HD_A
    r = jnp.arange(GATE_BLK)[:, None] // HD_A
    c = jnp.arange(GATE_BLK)[None, :] // HD_A

    def bd(w):
        rows = w.reshape(DEPTH, N_GATE_BLK, GATE_BLK, HD_A)
        return jnp.where(r == c, jnp.tile(rows, (1, 1, 1, per)), 0.0)

    return jnp.concatenate([bd(wa), bd(wi)], axis=-1)


def _toeplitz_tiles(w, n_rows):
    taps = w.shape[1]
    j = jnp.arange(n_rows)[:, None]
    t = jnp.arange(SUBLANES)[None, :]
    idx = j - t
    valid = (idx >= 0) & (idx < taps)
    tiles = w[:, jnp.clip(idx, 0, taps - 1), :]
    return jnp.where(valid[None, :, :, None], tiles, 0.0)


def _prep_params(norm_g, w_in, conv_a_w, conv_a_b, lru_wa, lru_ba, lru_wi, lru_bi, lru_lam, conv_b_w,
                 conv_b_b, ln_b_g, ln_b_b, ln_c_g, ln_c_b, gmlp_ws, gmlp_bs, w_out, final_g):
    row = lambda v: v.reshape(DEPTH, 1, -1)
    common = (
        row(norm_g), w_in, w_out, conv_a_w, row(conv_a_b),
        _block_diag_gates(lru_wa, lru_wi).astype(BF16), row(lru_ba), row(lru_bi), row(lru_lam),
        conv_b_w, row(conv_b_b), row(ln_b_g), row(ln_b_b), row(ln_c_g), row(ln_c_b),
        final_g.reshape(1, D_MODEL),
    )
    bs_full = jnp.repeat(jnp.transpose(gmlp_bs, (0, 2, 1)), HD_C, axis=2)
    wcol = jnp.repeat(jnp.transpose(gmlp_ws[:, :, :SUBLANES, :SUBLANES], (0, 3, 2, 1)), HD_C, axis=3)
    return {"common": common, "ws": gmlp_ws, "bs_full": bs_full, "wcol": wcol,
            "bs8": bs_full[:, :SUBLANES], "wta": _toeplitz_tiles(conv_a_w, HIST_A),
            "wtb": _toeplitz_tiles(conv_b_w, HIST_B + SUBLANES)}


def kernel(x_prompt, x_sample, c_prompt, c_sample, state_lru_conv, state_lru_h, state_ccm_conv, norm_g, w_ada,
           b_ada, w_in, conv_a_w, conv_a_b, lru_wa, lru_ba, lru_wi, lru_bi, lru_lam, conv_b_w, conv_b_b, ln_b_g,
           ln_b_b, ln_c_g, ln_c_b, gmlp_ws, gmlp_bs, w_out, final_g):
    bp = x_prompt.shape[0]
    mod_p, mod_s = _ada_call(c_prompt, c_sample, w_ada, b_ada)
    p = _prep_params(norm_g, w_in, conv_a_w, conv_a_b, lru_wa, lru_ba, lru_wi, lru_bi, lru_lam, conv_b_w,
                     conv_b_b, ln_b_g, ln_b_b, ln_c_g, ln_c_b, gmlp_ws, gmlp_bs, w_out, final_g)
    hist_a_s = jnp.transpose(state_lru_conv, (0, 2, 1, 3))
    hist_b_s = jnp.transpose(state_ccm_conv, (0, 2, 1, 3))
    xp, xs = x_prompt, x_sample
    conv_a_p, h_p, conv_b_p = [], [], []
    sample_state = None
    for l in range(DEPTH):
        xp, ha, hl, hb = _prompt_layer(l, xp, mod_p, p)
        conv_a_p.append(ha); h_p.append(hl.reshape(bp, W_A)); conv_b_p.append(hb)
        xs, sample_state = _sample_layer(l, xs, mod_s, hist_a_s, state_lru_h, hist_b_s, p, sample_state)
    new_a_s, new_h_s, new_b_s, new_v_s = sample_state
    return (xp, xs,
            jnp.stack(conv_a_p), jnp.stack(h_p), jnp.stack(conv_b_p),
            jnp.transpose(new_a_s, (0, 2, 1, 3)), new_h_s, jnp.transpose(new_b_s, (0, 2, 1, 3)), new_v_s)
```

```python
import functools

import jax
import jax.numpy as jnp
from jax import lax
from jax.experimental import pallas as pl
from jax.experimental.pallas import tpu as pltpu

F32 = jnp.float32
BF16 = jnp.bfloat16

D_MODEL = 2048
DEPTH = 4
MIX = D_MODEL
W_A = MIX // 2
H_A = 16
HD_A = W_A // H_A
CONV_A = 4
C_LRU = 8.0
W_B = MIX // 4
CONV_B = 31
W_C = MIX // 4
H_C = 8
HD_C = W_C // H_C
CHUNK = 128
EPS = 1e-6
IN_COLS = 2 * W_A + 3 * W_B + 3 * W_C

O_XA = 0
O_GA = O_XA + W_A
O_GLA = O_GA + W_A
O_GLB = O_GLA + W_B
O_GB = O_GLB + W_B
O_U = O_GB + W_B
O_V = O_U + W_C
O_GC = O_V + W_C

SUBLANES = 8
LANES = 128
GATE_BLK = 256
N_GATE_BLK = W_A // GATE_BLK
HIST_A = CONV_A - 1
HIST_B = CONV_B - 1
PAD_A = SUBLANES
PAD_B = 32
SLAB_ROWS = 64
N_STAGE = 4
DMA_QUEUES = 2
ROW_BLK = 64
SEQ_BLK = 4

VMEM_LIMIT = 58 * 1024 * 1024


def _dot(a, b):
    return jnp.dot(a, b, preferred_element_type=F32)


def _slab_copy(w_hbm, l, s, width, stage, sem):
    slot = s % N_STAGE
    return pltpu.make_async_copy(
        w_hbm.at[l, pl.ds(s * SLAB_ROWS, SLAB_ROWS), :], stage.at[slot, :, 0:width], sem.at[slot])


def _load_cast_weights(w_hbm, l, dst, stage, sem):
    rows, width = dst.shape
    n_slabs = rows // SLAB_ROWS
    ahead = N_STAGE - 1
    assert n_slabs % DMA_QUEUES == 0 and n_slabs > ahead
    for s in range(ahead):
        _slab_copy(w_hbm, l, s, width, stage, sem).start(priority=s % DMA_QUEUES)

    def body(i, carry):
        for q in range(DMA_QUEUES):
            s = i * DMA_QUEUES + q

            @pl.when(s + ahead < n_slabs)
            def _():
                _slab_copy(w_hbm, l, s + ahead, width, stage, sem).start(priority=(q + ahead) % DMA_QUEUES)

            _slab_copy(w_hbm, l, s, width, stage, sem).wait()
            r0 = pl.multiple_of(s * SLAB_ROWS, SLAB_ROWS)
            dst[pl.ds(r0, SLAB_ROWS), :] = stage[s % N_STAGE, :, 0:width].astype(BF16)
        return carry

    lax.fori_loop(0, n_slabs // DMA_QUEUES, body, 0)


def _load_layer_weights(l, win_hbm, wout_hbm, win_s, wout_s, stage, sem):
    _load_cast_weights(win_hbm, l, win_s, stage, sem)
    _load_cast_weights(wout_hbm, l, wout_s, stage, sem)


def _handoff_copies(win_src, wout_src, win_dst, wout_dst, sem):
    return (pltpu.make_async_copy(win_src, win_dst, sem.at[0]),
            pltpu.make_async_copy(wout_src, wout_dst, sem.at[1]))


def _sigmoid(x):
    return 0.5 * jnp.tanh(0.5 * x) + 0.5


def _silu(x):
    return x * _sigmoid(x)


def _layernorm(x, g, b):
    mu = jnp.mean(x, axis=-1, keepdims=True)
    xc = x - mu
    return xc * lax.rsqrt(jnp.mean(xc * xc, axis=-1, keepdims=True) + EPS) * g + b


def _rmsnorm(x, g):
    return (x * lax.rsqrt(jnp.mean(x * x, axis=-1, keepdims=True) + EPS)) * g


def _norm_mod(x, g, scale, shift):
    return _rmsnorm(x, g) * (1.0 + scale) + shift


def _lru_coeffs(xac, wg_ref, ba, bi, lam):
    xacb = xac.astype(BF16)
    pa, pi = [], []
    for j in range(N_GATE_BLK):
        g = _dot(xacb[:, j * GATE_BLK:(j + 1) * GATE_BLK], wg_ref[j])
        pa.append(g[:, :GATE_BLK])
        pi.append(g[:, GATE_BLK:])
    r = _sigmoid(jnp.concatenate(pa, axis=1) + ba)
    i = _sigmoid(jnp.concatenate(pi, axis=1) + bi)
    log_a = (-C_LRU * jax.nn.softplus(-lam)) * r
    a = jnp.exp(log_a)
    coef = jnp.sqrt(-jnp.tanh(log_a) * (1.0 + a * a))
    return a, coef * (i * xac)


def _group_scan(a3, b3):
    row = lax.broadcasted_iota(jnp.int32, (1,) + a3.shape[1:], 1)
    for s in (1, 2, 4):
        keep = row >= s
        a_prev = jnp.where(keep, pltpu.roll(a3, s, axis=1), 1.0)
        b_prev = jnp.where(keep, pltpu.roll(b3, s, axis=1), 0.0)
        b3 = a3 * b_prev + b3
        a3 = a3 * a_prev
    return a3, b3


def _tril_mask(n):
    r = lax.broadcasted_iota(jnp.int32, (n, n), 0)
    c = lax.broadcasted_iota(jnp.int32, (n, n), 1)
    return c <= r


def _win(win_ref, off, width):
    return win_ref[:, off:off + width]


def _ada_kernel(cp_ref, cs_ref, w_ref, b_ref, mp_ref, ms_ref):
    w = w_ref[0].astype(BF16)
    b = b_ref[0]
    mp_ref[0] = _dot(_silu(cp_ref[...]).astype(BF16), w) + b
    ms_ref[0] = _dot(_silu(cs_ref[...]).astype(BF16), w) + b


def _ada_call(c_prompt, c_sample, w_ada, b_ada, tn=1024):
    bp, bs = c_prompt.shape[0], c_sample.shape[0]
    n3 = w_ada.shape[-1]
    return pl.pallas_call(
        _ada_kernel,
        grid=(DEPTH, n3 // tn),
        in_specs=[
            pl.BlockSpec((bp, D_MODEL), lambda l, j: (0, 0)),
            pl.BlockSpec((bs, D_MODEL), lambda l, j: (0, 0)),
            pl.BlockSpec((1, D_MODEL, tn), lambda l, j: (l, 0, j)),
            pl.BlockSpec((1, 1, tn), lambda l, j: (l, 0, j)),
        ],
        out_specs=[
            pl.BlockSpec((1, bp, tn), lambda l, j: (l, 0, j)),
            pl.BlockSpec((1, bs, tn), lambda l, j: (l, 0, j)),
        ],
        out_shape=[
            jax.ShapeDtypeStruct((DEPTH, bp, n3), F32),
            jax.ShapeDtypeStruct((DEPTH, bs, n3), F32),
        ],
        compiler_params=pltpu.CompilerParams(
            dimension_semantics=("arbitrary", "arbitrary"), vmem_limit_bytes=VMEM_LIMIT),
        name="ada_mod",
    )(c_prompt, c_sample, w_ada, b_ada.reshape(DEPTH, 1, n3))


def _prompt_kernel(x_ref, mod_ref, ng_ref, win_hbm, wout_hbm, caw_ref, cab_ref, wg_ref, ba_ref, bi_ref,
                   lam_ref, cbw_ref, cbb_ref, lnbg_ref, lnbb_ref, lncg_ref, lncb_ref, fg_ref, ws_ref, bs_ref,
                   xo_ref, hista_ref, hlast_ref, histb_ref, win_bf, wout_bf,
                   win_ref, wout_ref, stage, sem, hsem, xa_buf, glu_buf, cs_buf, hc_ref, y_buf,
                   *, l, tl, final):
    b_idx = pl.program_id(0)
    c = pl.program_id(1)

    @pl.when((b_idx == 0) & (c == 0))
    def _():
        _load_layer_weights(l, win_hbm, wout_hbm, win_ref, wout_ref, stage, sem)
        for q, cp in enumerate(_handoff_copies(win_ref, wout_ref, win_bf, wout_bf, hsem)):
            cp.start(priority=q % DMA_QUEUES)

    @pl.when((b_idx == pl.num_programs(0) - 1) & (c == pl.num_programs(1) - 1))
    def _():
        for cp in _handoff_copies(win_ref, wout_ref, win_bf, wout_bf, hsem):
            cp.wait()

    @pl.when(c == 0)
    def _():
        xa_buf[0:PAD_A, :] = jnp.zeros((PAD_A, W_A), F32)
        glu_buf[0:PAD_B, :] = jnp.zeros((PAD_B, W_B), F32)
        hc_ref[...] = jnp.zeros_like(hc_ref)

    x = x_ref[0]
    shift = mod_ref[0, :, 0:D_MODEL]
    scale = mod_ref[0, :, D_MODEL:2 * D_MODEL]
    gate = mod_ref[0, :, 2 * D_MODEL:3 * D_MODEL]
    xnb = _norm_mod(x, ng_ref[...], scale, shift).astype(BF16)

    xa_buf[PAD_A:PAD_A + tl, :] = _dot(xnb, _win(win_ref, O_XA, W_A))
    xac = cab_ref[...]
    for k in range(CONV_A):
        o = PAD_A - HIST_A + k
        xac = xac + caw_ref[k:k + 1, :] * xa_buf[o:o + tl, :]
    new_hist_a = xa_buf[PAD_A + tl - HIST_A:PAD_A + tl, :]
    hista_ref[0] = new_hist_a
    xa_buf[PAD_A - HIST_A:PAD_A, :] = new_hist_a

    a, b = _lru_coeffs(xac, wg_ref, ba_ref[...], bi_ref[...], lam_ref[...])
    ng = tl // SUBLANES
    a3, b3 = _group_scan(a.reshape(ng, SUBLANES, W_A), b.reshape(ng, SUBLANES, W_A))
    carry = hc_ref[...]
    hs = []
    for g in range(ng):
        hg = b3[g] + a3[g] * carry
        carry = hg[SUBLANES - 1:SUBLANES, :]
        hs.append(hg)
    h = jnp.concatenate(hs, axis=0)
    hc_ref[...] = carry
    hlast_ref[0] = carry
    ga = _dot(xnb, _win(win_ref, O_GA, W_A))
    y_buf[:, 0:W_A] = (h * _silu(ga)).astype(BF16)

    gla = _dot(xnb, _win(win_ref, O_GLA, W_B))
    glb = _dot(xnb, _win(win_ref, O_GLB, W_B))
    glu_buf[PAD_B:PAD_B + tl, :] = gla * _sigmoid(glb)
    ext = tl + PAD_B - SUBLANES
    for r in range(1, SUBLANES):
        cs_buf[r - 1] = glu_buf[r:r + ext, :]
    gb = _dot(xnb, _win(win_ref, O_GB, W_B))
    for rb in range(tl // ROW_BLK):
        r0 = rb * ROW_BLK
        cb = cbb_ref[...]
        for k in range(CONV_B):
            q, r = divmod(k + PAD_B - HIST_B, SUBLANES)
            o = r0 + q * SUBLANES
            src = glu_buf[o:o + ROW_BLK, :] if r == 0 else cs_buf[r - 1, o:o + ROW_BLK, :]
            cb = cb + cbw_ref[k:k + 1, :] * src
        yb = _silu(_layernorm(cb, lnbg_ref[...], lnbb_ref[...])) * _silu(gb[r0:r0 + ROW_BLK, :])
        y_buf[r0:r0 + ROW_BLK, W_A:W_A + W_B] = yb.astype(BF16)
    new_hist_b = glu_buf[PAD_B + tl - HIST_B:PAD_B + tl, :]
    histb_ref[0] = new_hist_b
    glu_buf[PAD_B - HIST_B:PAD_B, :] = new_hist_b

    v = _dot(xnb, _win(win_ref, O_V, W_C))
    vnb = _layernorm(v, lncg_ref[...], lncb_ref[...]).astype(BF16)
    tri = _tril_mask(CHUNK)
    lane = lax.broadcasted_iota(jnp.int32, (CHUNK, LANES), 1)
    lo_half = lane < HD_C
    wpair = []
    for p in range(H_C // 2):
        w0 = jnp.where(tri, ws_ref[2 * p], 0.0).astype(BF16)
        w1 = jnp.where(tri, ws_ref[2 * p + 1], 0.0).astype(BF16)
        wpair.append(jnp.concatenate([w0, w1], axis=1))
    zero = jnp.zeros((CHUNK, LANES), BF16)
    chunks = []
    for cc in range(tl // CHUNK):
        cols = []
        for p in range(H_C // 2):
            vp = vnb[cc * CHUNK:(cc + 1) * CHUNK, p * LANES:(p + 1) * LANES]
            rhs = jnp.concatenate([jnp.where(lo_half, vp, zero), jnp.where(lo_half, zero, vp)], axis=0)
            cols.append(_dot(wpair[p], rhs))
        chunks.append(jnp.concatenate(cols, axis=1) + bs_ref[...])
    mixed = jnp.concatenate(chunks, axis=0)
    u = _dot(xnb, _win(win_ref, O_U, W_C))
    gc = _dot(xnb, _win(win_ref, O_GC, W_C))
    y_buf[:, W_A + W_B:MIX] = (u * mixed * _silu(gc)).astype(BF16)

    out = _dot(y_buf[...], wout_ref[...])
    xnew = x + gate * out
    if final:
        xnew = _rmsnorm(xnew, fg_ref[...])
    xo_ref[0] = xnew


def _layer_spec(l, shape):
    nd = len(shape)
    return pl.BlockSpec((None,) + tuple(shape), lambda *_: (l,) + (0,) * nd, pipeline_mode=pl.Buffered(1))


def _const_spec(shape):
    nd = len(shape)
    return pl.BlockSpec(tuple(shape), lambda *_: (0,) * nd, pipeline_mode=pl.Buffered(1))


def _weight_scratch():
    return [
        pltpu.VMEM((D_MODEL, IN_COLS), BF16),
        pltpu.VMEM((MIX, D_MODEL), BF16),
        pltpu.VMEM((N_STAGE, SLAB_ROWS, IN_COLS), F32),
        pltpu.SemaphoreType.DMA((N_STAGE,)),
    ]


def _layer_param_specs(l):
    return [
        _layer_spec(l, (1, D_MODEL)),
        pl.BlockSpec(memory_space=pl.ANY),
        pl.BlockSpec(memory_space=pl.ANY),
        _layer_spec(l, (CONV_A, W_A)),
        _layer_spec(l, (1, W_A)),
        _layer_spec(l, (N_GATE_BLK, GATE_BLK, 2 * GATE_BLK)),
        _layer_spec(l, (1, W_A)),
        _layer_spec(l, (1, W_A)),
        _layer_spec(l, (1, W_A)),
        _layer_spec(l, (CONV_B, W_B)),
        _layer_spec(l, (1, W_B)),
        _layer_spec(l, (1, W_B)),
        _layer_spec(l, (1, W_B)),
        _layer_spec(l, (1, W_C)),
        _layer_spec(l, (1, W_C)),
        _const_spec((1, D_MODEL)),
    ]


def _prompt_layer(l, x, mod_p, p, tl=256):
    bp, seq, _ = x.shape
    kern = functools.partial(_prompt_kernel, l=l, tl=tl, final=(l == DEPTH - 1))
    in_specs = [
        pl.BlockSpec((1, tl, D_MODEL), lambda b, c: (b, c, 0)),
        pl.BlockSpec((None, 1, 1, 3 * D_MODEL), lambda b, c: (l, b, 0, 0)),
    ] + _layer_param_specs(l) + [
        _layer_spec(l, (H_C, CHUNK, CHUNK)),
        _layer_spec(l, (CHUNK, W_C)),
    ]
    out_specs = [
        pl.BlockSpec((1, tl, D_MODEL), lambda b, c: (b, c, 0)),
        pl.BlockSpec((1, HIST_A, W_A), lambda b, c: (b, 0, 0)),
        pl.BlockSpec((1, 1, W_A), lambda b, c: (b, 0, 0)),
        pl.BlockSpec((1, HIST_B, W_B), lambda b, c: (b, 0, 0)),
        pl.BlockSpec(memory_space=pl.ANY),
        pl.BlockSpec(memory_space=pl.ANY),
    ]
    out_shape = [
        jax.ShapeDtypeStruct((bp, seq, D_MODEL), F32),
        jax.ShapeDtypeStruct((bp, HIST_A, W_A), F32),
        jax.ShapeDtypeStruct((bp, 1, W_A), F32),
        jax.ShapeDtypeStruct((bp, HIST_B, W_B), F32),
        jax.ShapeDtypeStruct((D_MODEL, IN_COLS), BF16),
        jax.ShapeDtypeStruct((MIX, D_MODEL), BF16),
    ]
    return pl.pallas_call(
        kern,
        grid=(bp, seq // tl),
        in_specs=in_specs,
        out_specs=out_specs,
        out_shape=out_shape,
        scratch_shapes=_weight_scratch() + [
            pltpu.SemaphoreType.DMA((2,)),
            pltpu.VMEM((PAD_A + tl, W_A), F32),
            pltpu.VMEM((PAD_B + tl, W_B), F32),
            pltpu.VMEM((SUBLANES - 1, tl + PAD_B - SUBLANES, W_B), F32),
            pltpu.VMEM((1, W_A), F32),
            pltpu.VMEM((tl, MIX), BF16),
        ],
        compiler_params=pltpu.CompilerParams(
            dimension_semantics=("arbitrary", "arbitrary"), vmem_limit_bytes=VMEM_LIMIT),
        name="prompt_layer",
    )(x, mod_p.reshape(DEPTH, bp, 1, 3 * D_MODEL), *p["common"], p["ws"], p["bs_full"])


def _rows_to_groups(get_row, n, width):
    rows = [jnp.broadcast_to(get_row(i), (SUBLANES, width)) for i in range(n)]
    return jnp.concatenate(rows, axis=0).reshape(n, SUBLANES, width)


def _rows_to_tokens(ref, nb, lo, hi):
    return _rows_to_groups(lambda i: ref[i:i + 1, lo:hi], nb, hi - lo).reshape(nb * SUBLANES, hi - lo)


def _sample_kernel(*refs, l, nb, final, n_alias):
    (x_ref, mod_ref, ng_ref, win_hbm, wout_hbm, caw_ref, cab_ref, wg_ref, ba_ref, bi_ref,
     lam_ref, cbw_ref, cbb_ref, lnbg_ref, lnbb_ref, lncg_ref, lncb_ref, fg_ref,
     wta_ref, wtb_ref, wcol_ref, bs8_ref, hista_in, h0_in, histb_in) = refs[:25]
    (xo_ref, hista_out, hlast_out, histb_out, vn_out,
     win_ref, wout_ref, sem, rows_buf, glu_rows, y_buf) = refs[25 + n_alias:]
    del cbw_ref, l

    @pl.when(pl.program_id(0) == 0)
    def _():
        copies = _handoff_copies(win_hbm, wout_hbm, win_ref, wout_ref, sem)
        for q, cp in enumerate(copies):
            cp.start(priority=q % DMA_QUEUES)
        for cp in copies:
            cp.wait()

    t = nb * SUBLANES
    x = x_ref[...].reshape(t, D_MODEL)
    shift = _rows_to_tokens(mod_ref, nb, 0, D_MODEL)
    scale = _rows_to_tokens(mod_ref, nb, D_MODEL, 2 * D_MODEL)
    xnb = _norm_mod(x, ng_ref[...], scale, shift).astype(BF16)

    xa3 = _dot(xnb, _win(win_ref, O_XA, W_A)).reshape(nb, SUBLANES, W_A)
    rows_buf[...] = xa3
    row = lax.broadcasted_iota(jnp.int32, (1, SUBLANES, W_A), 1)
    xac3 = cab_ref[...] + caw_ref[HIST_A:HIST_A + 1, :] * xa3
    for s in range(1, CONV_A):
        shifted = jnp.where(row >= s, pltpu.roll(xa3, s, axis=1), 0.0)
        xac3 = xac3 + caw_ref[HIST_A - s:HIST_A - s + 1, :] * shifted
    for j in range(HIST_A):
        hist3 = _rows_to_groups(lambda i, j=j: hista_in[j, i:i + 1, :], nb, W_A)
        xac3 = xac3 + wta_ref[j] * hist3
    for k in range(HIST_A):
        hista_out[k] = rows_buf[:, SUBLANES - HIST_A + k, :]
    a, b = _lru_coeffs(xac3.reshape(t, W_A), wg_ref, ba_ref[...], bi_ref[...], lam_ref[...])
    a3, b3 = _group_scan(a.reshape(nb, SUBLANES, W_A), b.reshape(nb, SUBLANES, W_A))
    h3 = b3 + a3 * _rows_to_groups(lambda i: h0_in[i:i + 1, :], nb, W_A)
    rows_buf[...] = h3
    hlast_out[...] = rows_buf[:, SUBLANES - 1, :]
    ga = _dot(xnb, _win(win_ref, O_GA, W_A))
    y_buf[:, 0:W_A] = (h3.reshape(t, W_A) * _silu(ga)).astype(BF16)

    gla = _dot(xnb, _win(win_ref, O_GLA, W_B))
    glb = _dot(xnb, _win(win_ref, O_GLB, W_B))
    glu3 = (gla * _sigmoid(glb)).reshape(nb, SUBLANES, W_B)
    glu_rows[...] = glu3
    gb = _dot(xnb, _win(win_ref, O_GB, W_B))
    for sb in range(nb // SEQ_BLK):
        s0 = sb * SEQ_BLK
        cb3 = cbb_ref[...]
        for j in range(HIST_B + SUBLANES):
            if j < HIST_B:
                rowj = _rows_to_groups(lambda i, j=j: histb_in[j, s0 + i:s0 + i + 1, :], SEQ_BLK, W_B)
            else:
                rowj = glu3[s0:s0 + SEQ_BLK, j - HIST_B:j - HIST_B + 1, :]
            cb3 = cb3 + wtb_ref[j] * rowj
        cb = cb3.reshape(SEQ_BLK * SUBLANES, W_B)
        r0 = s0 * SUBLANES
        yb = _silu(_layernorm(cb, lnbg_ref[...], lnbb_ref[...])) * _silu(gb[r0:r0 + SEQ_BLK * SUBLANES, :])
        y_buf[r0:r0 + SEQ_BLK * SUBLANES, W_A:W_A + W_B] = yb.astype(BF16)
    histb_out[0:HIST_B - SUBLANES] = histb_in[SUBLANES:HIST_B]
    for s in range(SUBLANES):
        histb_out[HIST_B - SUBLANES + s] = glu_rows[:, s, :]

    v = _dot(xnb, _win(win_ref, O_V, W_C))
    vn3 = _layernorm(v, lncg_ref[...], lncb_ref[...]).reshape(nb, SUBLANES, W_C)
    vn_out[...] = vn3
    trow = lax.broadcasted_iota(jnp.int32, (SUBLANES, W_C), 0)
    mixed3 = jnp.broadcast_to(bs8_ref[...], (nb, SUBLANES, W_C))
    for s in range(SUBLANES):
        wc = jnp.where(trow >= s, wcol_ref[s], 0.0)
        mixed3 = mixed3 + wc * vn3[:, s:s + 1, :]
    u = _dot(xnb, _win(win_ref, O_U, W_C))
    gc = _dot(xnb, _win(win_ref, O_GC, W_C))
    y_buf[:, W_A + W_B:MIX] = (u * mixed3.reshape(t, W_C) * _silu(gc)).astype(BF16)

    out = _dot(y_buf[...], wout_ref[...])
    gate = _rows_to_tokens(mod_ref, nb, 2 * D_MODEL, 3 * D_MODEL)
    xnew = x + gate * out
    if final:
        xnew = _rmsnorm(xnew, fg_ref[...])
    xo_ref[...] = xnew.reshape(nb, SUBLANES, D_MODEL)


def _sample_layer(l, x, mod_s, hist_a, h0, hist_b, p, wbf, prev, nb=16):
    bs, ls, _ = x.shape
    assert ls == SUBLANES
    out_shape = [
        jax.ShapeDtypeStruct((bs, SUBLANES, D_MODEL), F32),
        jax.ShapeDtypeStruct((DEPTH, HIST_A, bs, W_A), F32),
        jax.ShapeDtypeStruct((DEPTH, bs, W_A), F32),
        jax.ShapeDtypeStruct((DEPTH, HIST_B, bs, W_B), F32),
        jax.ShapeDtypeStruct((DEPTH, bs, SUBLANES, W_C), F32),
    ]
    if prev is None:
        prev = tuple(jnp.zeros(s.shape, s.dtype) for s in out_shape[1:])
    n_alias = len(prev)
    kern = functools.partial(_sample_kernel, l=l, nb=nb, final=(l == DEPTH - 1), n_alias=n_alias)
    in_specs = [
        pl.BlockSpec((nb, SUBLANES, D_MODEL), lambda i: (i, 0, 0)),
        pl.BlockSpec((None, nb, 3 * D_MODEL), lambda i: (l, i, 0)),
    ] + _layer_param_specs(l) + [
        _layer_spec(l, (HIST_A, SUBLANES, W_A)),
        _layer_spec(l, (HIST_B + SUBLANES, SUBLANES, W_B)),
        _layer_spec(l, (SUBLANES, SUBLANES, W_C)),
        _layer_spec(l, (SUBLANES, W_C)),
        pl.BlockSpec((None, HIST_A, nb, W_A), lambda i: (l, 0, i, 0)),
        pl.BlockSpec((None, nb, W_A), lambda i: (l, i, 0)),
        pl.BlockSpec((None, HIST_B, nb, W_B), lambda i: (l, 0, i, 0)),
    ] + [pl.BlockSpec(memory_space=pl.ANY)] * n_alias
    out_specs = [
        pl.BlockSpec((nb, SUBLANES, D_MODEL), lambda i: (i, 0, 0)),
        pl.BlockSpec((None, HIST_A, nb, W_A), lambda i: (l, 0, i, 0)),
        pl.BlockSpec((None, nb, W_A), lambda i: (l, i, 0)),
        pl.BlockSpec((None, HIST_B, nb, W_B), lambda i: (l, 0, i, 0)),
        pl.BlockSpec((None, nb, SUBLANES, W_C), lambda i: (l, i, 0, 0)),
    ]
    common = list(p["common"])
    common[1:3] = wbf
    args = [x, mod_s, *common, p["wta"], p["wtb"], p["wcol"], p["bs8"], hist_a, h0, hist_b]
    aliases = {len(args) + k: 1 + k for k in range(n_alias)}
    args += list(prev)
    outs = pl.pallas_call(
        kern,
        grid=(bs // nb,),
        in_specs=in_specs,
        out_specs=out_specs,
        out_shape=out_shape,
        input_output_aliases=aliases,
        scratch_shapes=[
            pltpu.VMEM((D_MODEL, IN_COLS), BF16),
            pltpu.VMEM((MIX, D_MODEL), BF16),
            pltpu.SemaphoreType.DMA((2,)),
            pltpu.VMEM((nb, SUBLANES, W_A), F32),
            pltpu.VMEM((nb, SUBLANES, W_B), F32),
            pltpu.VMEM((nb * SUBLANES, MIX), BF16),
        ],
        compiler_params=pltpu.CompilerParams(
            dimension_semantics=("arbitrary",), vmem_limit_bytes=VMEM_LIMIT),
        name="sample_layer",
    )(*args)
    return outs[0], tuple(outs[1:])


def _block_diag_gates(wa, wi):
    per = GATE_BLK // HD_A
    r = jnp.arange(GATE_BLK)[:, None] // HD_A
    c = jnp.arange(GATE_BLK)[None, :] // HD_A

    def bd(w):
        rows = w.reshape(DEPTH, N_GATE_BLK, GATE_BLK, HD_A)
        return jnp.where(r == c, jnp.tile(rows, (1, 1, 1, per)), 0.0)

    return jnp.concatenate([bd(wa), bd(wi)], axis=-1)


def _toeplitz_tiles(w, n_rows):
    taps = w.shape[1]
    j = jnp.arange(n_rows)[:, None]
    t = jnp.arange(SUBLANES)[None, :]
    idx = j - t
    valid = (idx >= 0) & (idx < taps)
    tiles = w[:, jnp.clip(idx, 0, taps - 1), :]
    return jnp.where(valid[None, :, :, None], tiles, 0.0)


def _prep_params(norm_g, w_in, conv_a_w, conv_a_b, lru_wa, lru_ba, lru_wi, lru_bi, lru_lam, conv_b_w,
                 conv_b_b, ln_b_g, ln_b_b, ln_c_g, ln_c_b, gmlp_ws, gmlp_bs, w_out, final_g):
    row = lambda v: v.reshape(DEPTH, 1, -1)
    common = (
        row(norm_g), w_in, w_out, conv_a_w, row(conv_a_b),
        _block_diag_gates(lru_wa, lru_wi).astype(BF16), row(lru_ba), row(lru_bi), row(lru_lam),
        conv_b_w, row(conv_b_b), row(ln_b_g), row(ln_b_b), row(ln_c_g), row(ln_c_b),
        final_g.reshape(1, D_MODEL),
    )
    bs_full = jnp.repeat(jnp.transpose(gmlp_bs, (0, 2, 1)), HD_C, axis=2)
    wcol = jnp.repeat(jnp.transpose(gmlp_ws[:, :, :SUBLANES, :SUBLANES], (0, 3, 2, 1)), HD_C, axis=3)
    return {"common": common, "ws": gmlp_ws, "bs_full": bs_full, "wcol": wcol,
            "bs8": bs_full[:, :SUBLANES], "wta": _toeplitz_tiles(conv_a_w, HIST_A),
            "wtb": _toeplitz_tiles(conv_b_w, HIST_B + SUBLANES)}


def kernel(x_prompt, x_sample, c_prompt, c_sample, state_lru_conv, state_lru_h, state_ccm_conv, norm_g, w_ada,
           b_ada, w_in, conv_a_w, conv_a_b, lru_wa, lru_ba, lru_wi, lru_bi, lru_lam, conv_b_w, conv_b_b, ln_b_g,
           ln_b_b, ln_c_g, ln_c_b, gmlp_ws, gmlp_bs, w_out, final_g):
    bp = x_prompt.shape[0]
    mod_p, mod_s = _ada_call(c_prompt, c_sample, w_ada, b_ada)
    p = _prep_params(norm_g, w_in, conv_a_w, conv_a_b, lru_wa, lru_ba, lru_wi, lru_bi, lru_lam, conv_b_w,
                     conv_b_b, ln_b_g, ln_b_b, ln_c_g, ln_c_b, gmlp_ws, gmlp_bs, w_out, final_g)
    hist_a_s = jnp.transpose(state_lru_conv, (0, 2, 1, 3))
    hist_b_s = jnp.transpose(state_ccm_conv, (0, 2, 1, 3))
    xp, xs = x_prompt, x_sample
    conv_a_p, h_p, conv_b_p = [], [], []
    sample_state = None
    for l in range(DEPTH):
        xp, ha, hl, hb, win_bf, wout_bf = _prompt_layer(l, xp, mod_p, p)
        conv_a_p.append(ha); h_p.append(hl.reshape(bp, W_A)); conv_b_p.append(hb)
        xs, sample_state = _sample_layer(l, xs, mod_s, hist_a_s, state_lru_h, hist_b_s, p,
                                         (win_bf, wout_bf), sample_state)
    new_a_s, new_h_s, new_b_s, new_v_s = sample_state
    return (xp, xs,
            jnp.stack(conv_a_p), jnp.stack(h_p), jnp.stack(conv_b_p),
            jnp.transpose(new_a_s, (0, 2, 1, 3)), new_h_s, jnp.transpose(new_b_s, (0, 2, 1, 3)), new_v_s)
```

```python
import functools

import jax
import jax.numpy as jnp
from jax import lax
from jax.experimental import pallas as pl
from jax.experimental.pallas import tpu as pltpu

F32 = jnp.float32
BF16 = jnp.bfloat16

D_MODEL = 2048
DEPTH = 4
MIX = D_MODEL
W_A = MIX // 2
H_A = 16
HD_A = W_A // H_A
CONV_A = 4
C_LRU = 8.0
W_B = MIX // 4
CONV_B = 31
W_C = MIX // 4
H_C = 8
HD_C = W_C // H_C
CHUNK = 128
EPS = 1e-6
IN_COLS = 2 * W_A + 3 * W_B + 3 * W_C

O_XA = 0
O_GA = O_XA + W_A
O_GLA = O_GA + W_A
O_GLB = O_GLA + W_B
O_GB = O_GLB + W_B
O_U = O_GB + W_B
O_V = O_U + W_C
O_GC = O_V + W_C

SUBLANES = 8
LANES = 128
GATE_BLK = 256
N_GATE_BLK = W_A // GATE_BLK
HIST_A = CONV_A - 1
HIST_B = CONV_B - 1
PAD_A = SUBLANES
PAD_B = 32
SLAB_ROWS = 64
N_STAGE = 4
ROW_BLK = 64
SEQ_BLK = 4

PROMPT_ROWS = 256
SAMPLE_SEQS = 16
ADA_COLS = 1024
VMEM_LIMIT = 58 * 1024 * 1024


def _dot(a, b):
    return jnp.dot(a, b, preferred_element_type=F32)


def _slab_copy(w_hbm, l, s, width, stage, sem):
    slot = s % N_STAGE
    return pltpu.make_async_copy(
        w_hbm.at[l, pl.ds(s * SLAB_ROWS, SLAB_ROWS), :], stage.at[slot, :, 0:width], sem.at[slot])


def _load_cast_weights(w_hbm, l, dst, stage, sem):
    rows, width = dst.shape
    n_slabs = rows // SLAB_ROWS
    ahead = N_STAGE - 1
    assert n_slabs > ahead
    for s in range(ahead):
        _slab_copy(w_hbm, l, s, width, stage, sem).start()

    def body(s, carry):
        @pl.when(s + ahead < n_slabs)
        def _():
            _slab_copy(w_hbm, l, s + ahead, width, stage, sem).start()

        _slab_copy(w_hbm, l, s, width, stage, sem).wait()
        r0 = pl.multiple_of(s * SLAB_ROWS, SLAB_ROWS)
        dst[pl.ds(r0, SLAB_ROWS), :] = stage[s % N_STAGE, :, 0:width].astype(BF16)
        return carry

    lax.fori_loop(0, n_slabs, body, 0)


def _load_layer_weights(l, win_hbm, wout_hbm, win_s, wout_s, stage, sem):
    _load_cast_weights(win_hbm, l, win_s, stage, sem)
    _load_cast_weights(wout_hbm, l, wout_s, stage, sem)


def _handoff_copies(win_src, wout_src, win_dst, wout_dst, sem):
    return (pltpu.make_async_copy(win_src, win_dst, sem.at[0]),
            pltpu.make_async_copy(wout_src, wout_dst, sem.at[1]))


def _sigmoid(x):
    return 0.5 * jnp.tanh(0.5 * x) + 0.5


def _silu(x):
    return x * _sigmoid(x)


def _layernorm(x, g, b):
    mu = jnp.mean(x, axis=-1, keepdims=True)
    xc = x - mu
    return xc * lax.rsqrt(jnp.mean(xc * xc, axis=-1, keepdims=True) + EPS) * g + b


def _rmsnorm(x, g):
    return (x * lax.rsqrt(jnp.mean(x * x, axis=-1, keepdims=True) + EPS)) * g


def _norm_mod(x, g, scale, shift):
    return _rmsnorm(x, g) * (1.0 + scale) + shift


def _lru_coeffs(xac, wg_ref, ba, bi, lam):
    xacb = xac.astype(BF16)
    pa, pi = [], []
    for j in range(N_GATE_BLK):
        g = _dot(xacb[:, j * GATE_BLK:(j + 1) * GATE_BLK], wg_ref[j])
        pa.append(g[:, :GATE_BLK])
        pi.append(g[:, GATE_BLK:])
    r = _sigmoid(jnp.concatenate(pa, axis=1) + ba)
    i = _sigmoid(jnp.concatenate(pi, axis=1) + bi)
    log_a = (-C_LRU * jax.nn.softplus(-lam)) * r
    a = jnp.exp(log_a)
    coef = jnp.sqrt(-jnp.tanh(log_a) * (1.0 + a * a))
    return a, coef * (i * xac)


def _group_scan(a3, b3):
    row = lax.broadcasted_iota(jnp.int32, (1,) + a3.shape[1:], 1)
    for s in (1, 2, 4):
        keep = row >= s
        a_prev = jnp.where(keep, pltpu.roll(a3, s, axis=1), 1.0)
        b_prev = jnp.where(keep, pltpu.roll(b3, s, axis=1), 0.0)
        b3 = a3 * b_prev + b3
        a3 = a3 * a_prev
    return a3, b3


def _tril_mask(n):
    r = lax.broadcasted_iota(jnp.int32, (n, n), 0)
    c = lax.broadcasted_iota(jnp.int32, (n, n), 1)
    return c <= r


def _win(win_ref, off, width):
    return win_ref[:, off:off + width]


def _ada_kernel(cp_ref, cs_ref, w_ref, b_ref, mp_ref, ms_ref):
    w = w_ref[0].astype(BF16)
    b = b_ref[0]
    mp_ref[0] = _dot(_silu(cp_ref[...]).astype(BF16), w) + b
    ms_ref[0] = _dot(_silu(cs_ref[...]).astype(BF16), w) + b


def _ada_call(c_prompt, c_sample, w_ada, b_ada):
    bp, bs = c_prompt.shape[0], c_sample.shape[0]
    n3 = w_ada.shape[-1]
    tn = ADA_COLS
    return pl.pallas_call(
        _ada_kernel,
        grid=(DEPTH, n3 // tn),
        in_specs=[
            pl.BlockSpec((bp, D_MODEL), lambda l, j: (0, 0)),
            pl.BlockSpec((bs, D_MODEL), lambda l, j: (0, 0)),
            pl.BlockSpec((1, D_MODEL, tn), lambda l, j: (l, 0, j)),
            pl.BlockSpec((1, 1, tn), lambda l, j: (l, 0, j)),
        ],
        out_specs=[
            pl.BlockSpec((1, bp, tn), lambda l, j: (l, 0, j)),
            pl.BlockSpec((1, bs, tn), lambda l, j: (l, 0, j)),
        ],
        out_shape=[
            jax.ShapeDtypeStruct((DEPTH, bp, n3), F32),
            jax.ShapeDtypeStruct((DEPTH, bs, n3), F32),
        ],
        compiler_params=pltpu.CompilerParams(
            dimension_semantics=("arbitrary", "arbitrary"), vmem_limit_bytes=VMEM_LIMIT),
        name="ada_mod",
    )(c_prompt, c_sample, w_ada, b_ada.reshape(DEPTH, 1, n3))


def _prompt_kernel(x_ref, mod_ref, ng_ref, win_hbm, wout_hbm, caw_ref, cab_ref, wg_ref, ba_ref, bi_ref,
                   lam_ref, cbb_ref, lnbg_ref, lnbb_ref, lncg_ref, lncb_ref, fg_ref, cbw_ref, ws_ref, bs_ref,
                   xo_ref, hista_ref, hlast_ref, histb_ref, win_bf, wout_bf,
                   win_ref, wout_ref, stage, sem, hsem, xa_buf, glu_buf, cs_buf, hc_ref, y_buf,
                   *, l, tl, final):
    b_idx = pl.program_id(0)
    c = pl.program_id(1)

    @pl.when((b_idx == 0) & (c == 0))
    def _():
        _load_layer_weights(l, win_hbm, wout_hbm, win_ref, wout_ref, stage, sem)
        for cp in _handoff_copies(win_ref, wout_ref, win_bf, wout_bf, hsem):
            cp.start()

    @pl.when((b_idx == pl.num_programs(0) - 1) & (c == pl.num_programs(1) - 1))
    def _():
        for cp in _handoff_copies(win_ref, wout_ref, win_bf, wout_bf, hsem):
            cp.wait()

    @pl.when(c == 0)
    def _():
        xa_buf[0:PAD_A, :] = jnp.zeros((PAD_A, W_A), F32)
        glu_buf[0:PAD_B, :] = jnp.zeros((PAD_B, W_B), F32)
        hc_ref[...] = jnp.zeros_like(hc_ref)

    x = x_ref[0]
    shift = mod_ref[0, :, 0:D_MODEL]
    scale = mod_ref[0, :, D_MODEL:2 * D_MODEL]
    gate = mod_ref[0, :, 2 * D_MODEL:3 * D_MODEL]
    xnb = _norm_mod(x, ng_ref[...], scale, shift).astype(BF16)

    xa_buf[PAD_A:PAD_A + tl, :] = _dot(xnb, _win(win_ref, O_XA, W_A))
    xac = cab_ref[...]
    for k in range(CONV_A):
        o = PAD_A - HIST_A + k
        xac = xac + caw_ref[k:k + 1, :] * xa_buf[o:o + tl, :]
    new_hist_a = xa_buf[PAD_A + tl - HIST_A:PAD_A + tl, :]
    hista_ref[0] = new_hist_a
    xa_buf[PAD_A - HIST_A:PAD_A, :] = new_hist_a
    a, b = _lru_coeffs(xac, wg_ref, ba_ref[...], bi_ref[...], lam_ref[...])

    gla = _dot(xnb, _win(win_ref, O_GLA, W_B))
    glb = _dot(xnb, _win(win_ref, O_GLB, W_B))
    glu_buf[PAD_B:PAD_B + tl, :] = gla * _sigmoid(glb)
    ext = tl + PAD_B - SUBLANES
    for r in range(1, SUBLANES):
        cs_buf[r - 1] = glu_buf[r:r + ext, :]

    ng = tl // SUBLANES
    a3, b3 = _group_scan(a.reshape(ng, SUBLANES, W_A), b.reshape(ng, SUBLANES, W_A))
    carry = hc_ref[...]
    hs = []
    for g in range(ng):
        hg = b3[g] + a3[g] * carry
        carry = hg[SUBLANES - 1:SUBLANES, :]
        hs.append(hg)
    h = jnp.concatenate(hs, axis=0)
    hc_ref[...] = carry
    hlast_ref[0] = carry
    ga = _dot(xnb, _win(win_ref, O_GA, W_A))
    y_buf[:, 0:W_A] = (h * _silu(ga)).astype(BF16)

    gb = _dot(xnb, _win(win_ref, O_GB, W_B))
    for rb in range(tl // ROW_BLK):
        r0 = rb * ROW_BLK
        cb = cbb_ref[...]
        for k in range(CONV_B):
            q, r = divmod(k + PAD_B - HIST_B, SUBLANES)
            o = r0 + q * SUBLANES
            src = glu_buf[o:o + ROW_BLK, :] if r == 0 else cs_buf[r - 1, o:o + ROW_BLK, :]
            cb = cb + cbw_ref[k:k + 1, :] * src
        yb = _silu(_layernorm(cb, lnbg_ref[...], lnbb_ref[...])) * _silu(gb[r0:r0 + ROW_BLK, :])
        y_buf[r0:r0 + ROW_BLK, W_A:W_A + W_B] = yb.astype(BF16)
    new_hist_b = glu_buf[PAD_B + tl - HIST_B:PAD_B + tl, :]
    histb_ref[0] = new_hist_b
    glu_buf[PAD_B - HIST_B:PAD_B, :] = new_hist_b

    v = _dot(xnb, _win(win_ref, O_V, W_C))
    vnb = _layernorm(v, lncg_ref[...], lncb_ref[...]).astype(BF16)
    tri = _tril_mask(CHUNK)
    lane = lax.broadcasted_iota(jnp.int32, (CHUNK, LANES), 1)
    lo_half = lane < HD_C
    wpair = []
    for p in range(H_C // 2):
        w0 = jnp.where(tri, ws_ref[2 * p], 0.0).astype(BF16)
        w1 = jnp.where(tri, ws_ref[2 * p + 1], 0.0).astype(BF16)
        wpair.append(jnp.concatenate([w0, w1], axis=1))
    zero = jnp.zeros((CHUNK, LANES), BF16)
    chunks = []
    for cc in range(tl // CHUNK):
        cols = []
        for p in range(H_C // 2):
            vp = vnb[cc * CHUNK:(cc + 1) * CHUNK, p * LANES:(p + 1) * LANES]
            rhs = jnp.concatenate([jnp.where(lo_half, vp, zero), jnp.where(lo_half, zero, vp)], axis=0)
            cols.append(_dot(wpair[p], rhs))
        chunks.append(jnp.concatenate(cols, axis=1) + bs_ref[...])
    mixed = jnp.concatenate(chunks, axis=0)
    u = _dot(xnb, _win(win_ref, O_U, W_C))
    gc = _dot(xnb, _win(win_ref, O_GC, W_C))
    y_buf[:, W_A + W_B:MIX] = (u * mixed * _silu(gc)).astype(BF16)

    out = _dot(y_buf[...], wout_ref[...])
    xnew = x + gate * out
    if final:
        xnew = _rmsnorm(xnew, fg_ref[...])
    xo_ref[0] = xnew


def _layer_spec(l, shape):
    nd = len(shape)
    return pl.BlockSpec((None,) + tuple(shape), lambda *_: (l,) + (0,) * nd, pipeline_mode=pl.Buffered(1))


def _const_spec(shape):
    nd = len(shape)
    return pl.BlockSpec(tuple(shape), lambda *_: (0,) * nd, pipeline_mode=pl.Buffered(1))


def _layer_param_specs(l):
    return [
        _layer_spec(l, (1, D_MODEL)),
        pl.BlockSpec(memory_space=pl.ANY),
        pl.BlockSpec(memory_space=pl.ANY),
        _layer_spec(l, (CONV_A, W_A)),
        _layer_spec(l, (1, W_A)),
        _layer_spec(l, (N_GATE_BLK, GATE_BLK, 2 * GATE_BLK)),
        _layer_spec(l, (1, W_A)),
        _layer_spec(l, (1, W_A)),
        _layer_spec(l, (1, W_A)),
        _layer_spec(l, (1, W_B)),
        _layer_spec(l, (1, W_B)),
        _layer_spec(l, (1, W_B)),
        _layer_spec(l, (1, W_C)),
        _layer_spec(l, (1, W_C)),
        _const_spec((1, D_MODEL)),
    ]


def _prompt_layer(l, x, mod_p, p):
    bp, seq, _ = x.shape
    tl = PROMPT_ROWS
    kern = functools.partial(_prompt_kernel, l=l, tl=tl, final=(l == DEPTH - 1))
    in_specs = [
        pl.BlockSpec((1, tl, D_MODEL), lambda b, c: (b, c, 0)),
        pl.BlockSpec((None, 1, 1, 3 * D_MODEL), lambda b, c: (l, b, 0, 0)),
    ] + _layer_param_specs(l) + [
        _layer_spec(l, (CONV_B, W_B)),
        _layer_spec(l, (H_C, CHUNK, CHUNK)),
        _layer_spec(l, (CHUNK, W_C)),
    ]
    out_specs = [
        pl.BlockSpec((1, tl, D_MODEL), lambda b, c: (b, c, 0)),
        pl.BlockSpec((1, HIST_A, W_A), lambda b, c: (b, 0, 0)),
        pl.BlockSpec((1, 1, W_A), lambda b, c: (b, 0, 0)),
        pl.BlockSpec((1, HIST_B, W_B), lambda b, c: (b, 0, 0)),
        pl.BlockSpec(memory_space=pl.ANY),
        pl.BlockSpec(memory_space=pl.ANY),
    ]
    out_shape = [
        jax.ShapeDtypeStruct((bp, seq, D_MODEL), F32),
        jax.ShapeDtypeStruct((bp, HIST_A, W_A), F32),
        jax.ShapeDtypeStruct((bp, 1, W_A), F32),
        jax.ShapeDtypeStruct((bp, HIST_B, W_B), F32),
        jax.ShapeDtypeStruct((D_MODEL, IN_COLS), BF16),
        jax.ShapeDtypeStruct((MIX, D_MODEL), BF16),
    ]
    return pl.pallas_call(
        kern,
        grid=(bp, seq // tl),
        in_specs=in_specs,
        out_specs=out_specs,
        out_shape=out_shape,
        scratch_shapes=[
            pltpu.VMEM((D_MODEL, IN_COLS), BF16),
            pltpu.VMEM((MIX, D_MODEL), BF16),
            pltpu.VMEM((N_STAGE, SLAB_ROWS, IN_COLS), F32),
            pltpu.SemaphoreType.DMA((N_STAGE,)),
            pltpu.SemaphoreType.DMA((2,)),
            pltpu.VMEM((PAD_A + tl, W_A), F32),
            pltpu.VMEM((PAD_B + tl, W_B), F32),
            pltpu.VMEM((SUBLANES - 1, tl + PAD_B - SUBLANES, W_B), F32),
            pltpu.VMEM((1, W_A), F32),
            pltpu.VMEM((tl, MIX), BF16),
        ],
        compiler_params=pltpu.CompilerParams(
            dimension_semantics=("arbitrary", "arbitrary"), vmem_limit_bytes=VMEM_LIMIT),
        name="prompt_layer",
    )(x, mod_p.reshape(DEPTH, bp, 1, 3 * D_MODEL), *p["common"], p["conv_b_w"], p["ws"], p["bs_full"])


def _rows_to_groups(get_row, n, width):
    rows = [jnp.broadcast_to(get_row(i), (SUBLANES, width)) for i in range(n)]
    return jnp.concatenate(rows, axis=0).reshape(n, SUBLANES, width)


def _rows_to_tokens(ref, nb, lo, hi):
    return _rows_to_groups(lambda i: ref[i:i + 1, lo:hi], nb, hi - lo).reshape(nb * SUBLANES, hi - lo)


N_SAMPLE_INPUTS = 24


def _sample_kernel(*refs, nb, final, n_alias):
    (x_ref, mod_ref, ng_ref, win_hbm, wout_hbm, caw_ref, cab_ref, wg_ref, ba_ref, bi_ref,
     lam_ref, cbb_ref, lnbg_ref, lnbb_ref, lncg_ref, lncb_ref, fg_ref,
     wta_ref, wtb_ref, wcol_ref, bs8_ref, hista_in, h0_in, histb_in) = refs[:N_SAMPLE_INPUTS]
    (xo_ref, hista_out, hlast_out, histb_out, vn_out,
     win_ref, wout_ref, sem, rows_buf, glu_rows, y_buf) = refs[N_SAMPLE_INPUTS + n_alias:]

    @pl.when(pl.program_id(0) == 0)
    def _():
        copies = _handoff_copies(win_hbm, wout_hbm, win_ref, wout_ref, sem)
        for cp in copies:
            cp.start()
        for cp in copies:
            cp.wait()

    t = nb * SUBLANES
    x = x_ref[...].reshape(t, D_MODEL)
    shift = _rows_to_tokens(mod_ref, nb, 0, D_MODEL)
    scale = _rows_to_tokens(mod_ref, nb, D_MODEL, 2 * D_MODEL)
    xnb = _norm_mod(x, ng_ref[...], scale, shift).astype(BF16)

    xa3 = _dot(xnb, _win(win_ref, O_XA, W_A)).reshape(nb, SUBLANES, W_A)
    rows_buf[...] = xa3
    row = lax.broadcasted_iota(jnp.int32, (1, SUBLANES, W_A), 1)
    xac3 = cab_ref[...] + caw_ref[HIST_A:HIST_A + 1, :] * xa3
    for s in range(1, CONV_A):
        shifted = jnp.where(row >= s, pltpu.roll(xa3, s, axis=1), 0.0)
        xac3 = xac3 + caw_ref[HIST_A - s:HIST_A - s + 1, :] * shifted
    for j in range(HIST_A):
        hist3 = _rows_to_groups(lambda i, j=j: hista_in[j, i:i + 1, :], nb, W_A)
        xac3 = xac3 + wta_ref[j] * hist3
    for k in range(HIST_A):
        hista_out[k] = rows_buf[:, SUBLANES - HIST_A + k, :]
    a, b = _lru_coeffs(xac3.reshape(t, W_A), wg_ref, ba_ref[...], bi_ref[...], lam_ref[...])
    a3, b3 = _group_scan(a.reshape(nb, SUBLANES, W_A), b.reshape(nb, SUBLANES, W_A))
    h3 = b3 + a3 * _rows_to_groups(lambda i: h0_in[i:i + 1, :], nb, W_A)
    rows_buf[...] = h3
    hlast_out[...] = rows_buf[:, SUBLANES - 1, :]
    ga = _dot(xnb, _win(win_ref, O_GA, W_A))
    y_buf[:, 0:W_A] = (h3.reshape(t, W_A) * _silu(ga)).astype(BF16)

    gla = _dot(xnb, _win(win_ref, O_GLA, W_B))
    glb = _dot(xnb, _win(win_ref, O_GLB, W_B))
    glu3 = (gla * _sigmoid(glb)).reshape(nb, SUBLANES, W_B)
    glu_rows[...] = glu3
    gb = _dot(xnb, _win(win_ref, O_GB, W_B))
    for sb in range(nb // SEQ_BLK):
        s0 = sb * SEQ_BLK
        cb3 = cbb_ref[...]
        for j in range(HIST_B + SUBLANES):
            if j < HIST_B:
                rowj = _rows_to_groups(lambda i, j=j: histb_in[j, s0 + i:s0 + i + 1, :], SEQ_BLK, W_B)
            else:
                rowj = glu3[s0:s0 + SEQ_BLK, j - HIST_B:j - HIST_B + 1, :]
            cb3 = cb3 + wtb_ref[j] * rowj
        cb = cb3.reshape(SEQ_BLK * SUBLANES, W_B)
        r0 = s0 * SUBLANES
        yb = _silu(_layernorm(cb, lnbg_ref[...], lnbb_ref[...])) * _silu(gb[r0:r0 + SEQ_BLK * SUBLANES, :])
        y_buf[r0:r0 + SEQ_BLK * SUBLANES, W_A:W_A + W_B] = yb.astype(BF16)
    histb_out[0:HIST_B - SUBLANES] = histb_in[SUBLANES:HIST_B]
    for s in range(SUBLANES):
        histb_out[HIST_B - SUBLANES + s] = glu_rows[:, s, :]

    v = _dot(xnb, _win(win_ref, O_V, W_C))
    vn3 = _layernorm(v, lncg_ref[...], lncb_ref[...]).reshape(nb, SUBLANES, W_C)
    vn_out[...] = vn3
    trow = lax.broadcasted_iota(jnp.int32, (SUBLANES, W_C), 0)
    mixed3 = jnp.broadcast_to(bs8_ref[...], (nb, SUBLANES, W_C))
    for s in range(SUBLANES):
        wc = jnp.where(trow >= s, wcol_ref[s], 0.0)
        mixed3 = mixed3 + wc * vn3[:, s:s + 1, :]
    u = _dot(xnb, _win(win_ref, O_U, W_C))
    gc = _dot(xnb, _win(win_ref, O_GC, W_C))
    y_buf[:, W_A + W_B:MIX] = (u * mixed3.reshape(t, W_C) * _silu(gc)).astype(BF16)

    out = _dot(y_buf[...], wout_ref[...])
    gate = _rows_to_tokens(mod_ref, nb, 2 * D_MODEL, 3 * D_MODEL)
    xnew = x + gate * out
    if final:
        xnew = _rmsnorm(xnew, fg_ref[...])
    xo_ref[...] = xnew.reshape(nb, SUBLANES, D_MODEL)


def _sample_layer(l, x, mod_s, hist_a, h0, hist_b, p, wbf, prev):
    bs, ls, _ = x.shape
    assert ls == SUBLANES
    nb = SAMPLE_SEQS
    out_shape = [
        jax.ShapeDtypeStruct((bs, SUBLANES, D_MODEL), F32),
        jax.ShapeDtypeStruct((DEPTH, HIST_A, bs, W_A), F32),
        jax.ShapeDtypeStruct((DEPTH, bs, W_A), F32),
        jax.ShapeDtypeStruct((DEPTH, HIST_B, bs, W_B), F32),
        jax.ShapeDtypeStruct((DEPTH, bs, SUBLANES, W_C), F32),
    ]
    if prev is None:
        prev = tuple(jnp.zeros(s.shape, s.dtype) for s in out_shape[1:])
    n_alias = len(prev)
    kern = functools.partial(_sample_kernel, nb=nb, final=(l == DEPTH - 1), n_alias=n_alias)
    in_specs = [
        pl.BlockSpec((nb, SUBLANES, D_MODEL), lambda i: (i, 0, 0)),
        pl.BlockSpec((None, nb, 3 * D_MODEL), lambda i: (l, i, 0)),
    ] + _layer_param_specs(l) + [
        _layer_spec(l, (HIST_A, SUBLANES, W_A)),
        _layer_spec(l, (HIST_B + SUBLANES, SUBLANES, W_B)),
        _layer_spec(l, (SUBLANES, SUBLANES, W_C)),
        _layer_spec(l, (SUBLANES, W_C)),
        pl.BlockSpec((None, HIST_A, nb, W_A), lambda i: (l, 0, i, 0)),
        pl.BlockSpec((None, nb, W_A), lambda i: (l, i, 0)),
        pl.BlockSpec((None, HIST_B, nb, W_B), lambda i: (l, 0, i, 0)),
    ] + [pl.BlockSpec(memory_space=pl.ANY)] * n_alias
    out_specs = [
        pl.BlockSpec((nb, SUBLANES, D_MODEL), lambda i: (i, 0, 0)),
        pl.BlockSpec((None, HIST_A, nb, W_A), lambda i: (l, 0, i, 0)),
        pl.BlockSpec((None, nb, W_A), lambda i: (l, i, 0)),
        pl.BlockSpec((None, HIST_B, nb, W_B), lambda i: (l, 0, i, 0)),
        pl.BlockSpec((None, nb, SUBLANES, W_C), lambda i: (l, i, 0, 0)),
    ]
    common = list(p["common"])
    common[1:3] = wbf
    args = [x, mod_s, *common, p["wta"], p["wtb"], p["wcol"], p["bs8"], hist_a, h0, hist_b]
    assert len(args) == N_SAMPLE_INPUTS
    aliases = {len(args) + k: 1 + k for k in range(n_alias)}
    args += list(prev)
    outs = pl.pallas_call(
        kern,
        grid=(bs // nb,),
        in_specs=in_specs,
        out_specs=out_specs,
        out_shape=out_shape,
        input_output_aliases=aliases,
        scratch_shapes=[
            pltpu.VMEM((D_MODEL, IN_COLS), BF16),
            pltpu.VMEM((MIX, D_MODEL), BF16),
            pltpu.SemaphoreType.DMA((2,)),
            pltpu.VMEM((nb, SUBLANES, W_A), F32),
            pltpu.VMEM((nb, SUBLANES, W_B), F32),
            pltpu.VMEM((nb * SUBLANES, MIX), BF16),
        ],
        compiler_params=pltpu.CompilerParams(
            dimension_semantics=("arbitrary",), vmem_limit_bytes=VMEM_LIMIT),
        name="sample_layer",
    )(*args)
    return outs[0], tuple(outs[1:])


def _block_diag_gates(wa, wi):
    per = GATE_BLK // HD_A
    r = jnp.arange(GATE_BLK)[:, None] // HD_A
    c = jnp.arange(GATE_BLK)[None, :] // HD_A

    def bd(w):
        rows = w.reshape(DEPTH, N_GATE_BLK, GATE_BLK, HD_A)
        return jnp.where(r == c, jnp.tile(rows, (1, 1, 1, per)), 0.0)

    return jnp.concatenate([bd(wa), bd(wi)], axis=-1)


def _toeplitz_tiles(w, n_rows):
    taps = w.shape[1]
    j = jnp.arange(n_rows)[:, None]
    t = jnp.arange(SUBLANES)[None, :]
    idx = j - t
    valid = (idx >= 0) & (idx < taps)
    tiles = w[:, jnp.clip(idx, 0, taps - 1), :]
    return jnp.where(valid[None, :, :, None], tiles, 0.0)


def _prep_params(norm_g, w_in, conv_a_w, conv_a_b, lru_wa, lru_ba, lru_wi, lru_bi, lru_lam, conv_b_w,
                 conv_b_b, ln_b_g, ln_b_b, ln_c_g, ln_c_b, gmlp_ws, gmlp_bs, w_out, final_g):
    row = lambda v: v.reshape(DEPTH, 1, -1)
    common = (
        row(norm_g), w_in, w_out, conv_a_w, row(conv_a_b),
        _block_diag_gates(lru_wa, lru_wi).astype(BF16), row(lru_ba), row(lru_bi), row(lru_lam),
        row(conv_b_b), row(ln_b_g), row(ln_b_b), row(ln_c_g), row(ln_c_b),
        final_g.reshape(1, D_MODEL),
    )
    bs_full = jnp.repeat(jnp.transpose(gmlp_bs, (0, 2, 1)), HD_C, axis=2)
    wcol = jnp.repeat(jnp.transpose(gmlp_ws[:, :, :SUBLANES, :SUBLANES], (0, 3, 2, 1)), HD_C, axis=3)
    return {"common": common, "conv_b_w": conv_b_w, "ws": gmlp_ws, "bs_full": bs_full, "wcol": wcol,
            "bs8": bs_full[:, :SUBLANES], "wta": _toeplitz_tiles(conv_a_w, HIST_A),
            "wtb": _toeplitz_tiles(conv_b_w, HIST_B + SUBLANES)}


def kernel(x_prompt, x_sample, c_prompt, c_sample, state_lru_conv, state_lru_h, state_ccm_conv, norm_g, w_ada,
           b_ada, w_in, conv_a_w, conv_a_b, lru_wa, lru_ba, lru_wi, lru_bi, lru_lam, conv_b_w, conv_b_b, ln_b_g,
           ln_b_b, ln_c_g, ln_c_b, gmlp_ws, gmlp_bs, w_out, final_g):
    bp = x_prompt.shape[0]
    mod_p, mod_s = _ada_call(c_prompt, c_sample, w_ada, b_ada)
    p = _prep_params(norm_g, w_in, conv_a_w, conv_a_b, lru_wa, lru_ba, lru_wi, lru_bi, lru_lam, conv_b_w,
                     conv_b_b, ln_b_g, ln_b_b, ln_c_g, ln_c_b, gmlp_ws, gmlp_bs, w_out, final_g)
    hist_a_s = jnp.transpose(state_lru_conv, (0, 2, 1, 3))
    hist_b_s = jnp.transpose(state_ccm_conv, (0, 2, 1, 3))
    xp, xs = x_prompt, x_sample
    conv_a_p, h_p, conv_b_p = [], [], []
    sample_state = None
    for l in range(DEPTH):
        xp, ha, hl, hb, win_bf, wout_bf = _prompt_layer(l, xp, mod_p, p)
        conv_a_p.append(ha); h_p.append(hl.reshape(bp, W_A)); conv_b_p.append(hb)
        xs, sample_state = _sample_layer(l, xs, mod_s, hist_a_s, state_lru_h, hist_b_s, p,
                                         (win_bf, wout_bf), sample_state)
    new_a_s, new_h_s, new_b_s, new_v_s = sample_state
    return (xp, xs,
            jnp.stack(conv_a_p), jnp.stack(h_p), jnp.stack(conv_b_p),
            jnp.transpose(new_a_s, (0, 2, 1, 3)), new_h_s, jnp.transpose(new_b_s, (0, 2, 1, 3)), new_v_s)
```

```python
import functools

import jax
import jax.numpy as jnp
from jax import lax
from jax.experimental import pallas as pl
from jax.experimental.pallas import tpu as pltpu

F32 = jnp.float32
BF16 = jnp.bfloat16

D_MODEL = 2048
DEPTH = 4
MIX = D_MODEL
W_A = MIX // 2
H_A = 16
HD_A = W_A // H_A
CONV_A = 4
C_LRU = 8.0
W_B = MIX // 4
CONV_B = 31
W_C = MIX // 4
H_C = 8
HD_C = W_C // H_C
CHUNK = 128
EPS = 1e-6
IN_COLS = 2 * W_A + 3 * W_B + 3 * W_C

O_XA = 0
O_GA = O_XA + W_A
O_GLA = O_GA + W_A
O_GLB = O_GLA + W_B
O_GB = O_GLB + W_B
O_U = O_GB + W_B
O_V = O_U + W_C
O_GC = O_V + W_C

SUBLANES = 8
LANES = 128
GATE_BLK = 256
N_GATE_BLK = W_A // GATE_BLK
HIST_A = CONV_A - 1
HIST_B = CONV_B - 1
PAD_A = SUBLANES
PAD_B = 32
SLAB_ROWS = 64
N_STAGE = 8
ROW_BLK = 64
SEQ_BLK = 4

PROMPT_ROWS = 256
SAMPLE_SEQS = 16
ADA_COLS = 1024
VMEM_LIMIT = 58 * 1024 * 1024


def _dot(a, b):
    return jnp.dot(a, b, preferred_element_type=F32)


def _slab_copy(w_hbm, l, s, width, stage, sem):
    slot = s % N_STAGE
    return pltpu.make_async_copy(
        w_hbm.at[l, pl.ds(s * SLAB_ROWS, SLAB_ROWS), :], stage.at[slot, :, 0:width], sem.at[slot])


def _load_cast_weights(w_hbm, l, dst, stage, sem):
    rows, width = dst.shape
    n_slabs = rows // SLAB_ROWS
    ahead = N_STAGE - 1
    assert n_slabs > ahead
    for s in range(ahead):
        _slab_copy(w_hbm, l, s, width, stage, sem).start()

    def body(s, carry):
        @pl.when(s + ahead < n_slabs)
        def _():
            _slab_copy(w_hbm, l, s + ahead, width, stage, sem).start()

        _slab_copy(w_hbm, l, s, width, stage, sem).wait()
        r0 = pl.multiple_of(s * SLAB_ROWS, SLAB_ROWS)
        dst[pl.ds(r0, SLAB_ROWS), :] = stage[s % N_STAGE, :, 0:width].astype(BF16)
        return carry

    lax.fori_loop(0, n_slabs, body, 0)


def _load_layer_weights(l, win_hbm, wout_hbm, win_s, wout_s, stage, sem):
    _load_cast_weights(win_hbm, l, win_s, stage, sem)
    _load_cast_weights(wout_hbm, l, wout_s, stage, sem)


def _handoff_copies(win_src, wout_src, win_dst, wout_dst, sem):
    return (pltpu.make_async_copy(win_src, win_dst, sem.at[0]),
            pltpu.make_async_copy(wout_src, wout_dst, sem.at[1]))


def _sigmoid(x):
    return 0.5 * jnp.tanh(0.5 * x) + 0.5


def _silu(x):
    return x * _sigmoid(x)


def _layernorm(x, g, b):
    mu = jnp.mean(x, axis=-1, keepdims=True)
    xc = x - mu
    return xc * lax.rsqrt(jnp.mean(xc * xc, axis=-1, keepdims=True) + EPS) * g + b


def _rmsnorm(x, g):
    return (x * lax.rsqrt(jnp.mean(x * x, axis=-1, keepdims=True) + EPS)) * g


def _norm_mod(x, g, scale, shift):
    return _rmsnorm(x, g) * (1.0 + scale) + shift


def _lru_coeffs(xac, wg_ref, ba, bi, lam):
    xacb = xac.astype(BF16)
    pa, pi = [], []
    for j in range(N_GATE_BLK):
        g = _dot(xacb[:, j * GATE_BLK:(j + 1) * GATE_BLK], wg_ref[j])
        pa.append(g[:, :GATE_BLK])
        pi.append(g[:, GATE_BLK:])
    r = _sigmoid(jnp.concatenate(pa, axis=1) + ba)
    i = _sigmoid(jnp.concatenate(pi, axis=1) + bi)
    log_a = (-C_LRU * jax.nn.softplus(-lam)) * r
    a = jnp.exp(log_a)
    coef = jnp.sqrt(-jnp.tanh(log_a) * (1.0 + a * a))
    return a, coef * (i * xac)


def _group_scan(a3, b3):
    row = lax.broadcasted_iota(jnp.int32, (1,) + a3.shape[1:], 1)
    for s in (1, 2, 4):
        keep = row >= s
        a_prev = jnp.where(keep, pltpu.roll(a3, s, axis=1), 1.0)
        b_prev = jnp.where(keep, pltpu.roll(b3, s, axis=1), 0.0)
        b3 = a3 * b_prev + b3
        a3 = a3 * a_prev
    return a3, b3


def _tril_mask(n):
    r = lax.broadcasted_iota(jnp.int32, (n, n), 0)
    c = lax.broadcasted_iota(jnp.int32, (n, n), 1)
    return c <= r


def _win(win_ref, off, width):
    return win_ref[:, off:off + width]


def _ada_kernel(cp_ref, cs_ref, w_ref, b_ref, mp_ref, ms_ref):
    w = w_ref[0].astype(BF16)
    b = b_ref[0]
    mp_ref[0] = _dot(_silu(cp_ref[...]).astype(BF16), w) + b
    ms_ref[0] = _dot(_silu(cs_ref[...]).astype(BF16), w) + b


def _ada_call(c_prompt, c_sample, w_ada, b_ada):
    bp, bs = c_prompt.shape[0], c_sample.shape[0]
    n3 = w_ada.shape[-1]
    tn = ADA_COLS
    return pl.pallas_call(
        _ada_kernel,
        grid=(DEPTH, n3 // tn),
        in_specs=[
            pl.BlockSpec((bp, D_MODEL), lambda l, j: (0, 0)),
            pl.BlockSpec((bs, D_MODEL), lambda l, j: (0, 0)),
            pl.BlockSpec((1, D_MODEL, tn), lambda l, j: (l, 0, j)),
            pl.BlockSpec((1, 1, tn), lambda l, j: (l, 0, j)),
        ],
        out_specs=[
            pl.BlockSpec((1, bp, tn), lambda l, j: (l, 0, j)),
            pl.BlockSpec((1, bs, tn), lambda l, j: (l, 0, j)),
        ],
        out_shape=[
            jax.ShapeDtypeStruct((DEPTH, bp, n3), F32),
            jax.ShapeDtypeStruct((DEPTH, bs, n3), F32),
        ],
        compiler_params=pltpu.CompilerParams(
            dimension_semantics=("arbitrary", "arbitrary"), vmem_limit_bytes=VMEM_LIMIT),
        name="ada_mod",
    )(c_prompt, c_sample, w_ada, b_ada.reshape(DEPTH, 1, n3))


def _prompt_kernel(x_ref, mod_ref, ng_ref, win_hbm, wout_hbm, caw_ref, cab_ref, wg_ref, ba_ref, bi_ref,
                   lam_ref, cbb_ref, lnbg_ref, lnbb_ref, lncg_ref, lncb_ref, fg_ref, cbw_ref, ws_ref, bs_ref,
                   xo_ref, hista_ref, hlast_ref, histb_ref, win_bf, wout_bf,
                   win_ref, wout_ref, stage, sem, hsem, xa_buf, glu_buf, cs_buf, hc_ref, y_buf,
                   *, l, tl, final):
    b_idx = pl.program_id(0)
    c = pl.program_id(1)

    @pl.when((b_idx == 0) & (c == 0))
    def _():
        _load_layer_weights(l, win_hbm, wout_hbm, win_ref, wout_ref, stage, sem)
        for cp in _handoff_copies(win_ref, wout_ref, win_bf, wout_bf, hsem):
            cp.start()

    @pl.when((b_idx == pl.num_programs(0) - 1) & (c == pl.num_programs(1) - 1))
    def _():
        for cp in _handoff_copies(win_ref, wout_ref, win_bf, wout_bf, hsem):
            cp.wait()

    @pl.when(c == 0)
    def _():
        xa_buf[0:PAD_A, :] = jnp.zeros((PAD_A, W_A), F32)
        glu_buf[0:PAD_B, :] = jnp.zeros((PAD_B, W_B), F32)
        hc_ref[...] = jnp.zeros_like(hc_ref)

    x = x_ref[0]
    shift = mod_ref[0, :, 0:D_MODEL]
    scale = mod_ref[0, :, D_MODEL:2 * D_MODEL]
    gate = mod_ref[0, :, 2 * D_MODEL:3 * D_MODEL]
    xnb = _norm_mod(x, ng_ref[...], scale, shift).astype(BF16)

    xa_buf[PAD_A:PAD_A + tl, :] = _dot(xnb, _win(win_ref, O_XA, W_A))
    xac = cab_ref[...]
    for k in range(CONV_A):
        o = PAD_A - HIST_A + k
        xac = xac + caw_ref[k:k + 1, :] * xa_buf[o:o + tl, :]
    new_hist_a = xa_buf[PAD_A + tl - HIST_A:PAD_A + tl, :]
    hista_ref[0] = new_hist_a
    xa_buf[PAD_A - HIST_A:PAD_A, :] = new_hist_a
    a, b = _lru_coeffs(xac, wg_ref, ba_ref[...], bi_ref[...], lam_ref[...])

    gla = _dot(xnb, _win(win_ref, O_GLA, W_B))
    glb = _dot(xnb, _win(win_ref, O_GLB, W_B))
    glu_buf[PAD_B:PAD_B + tl, :] = gla * _sigmoid(glb)
    ext = tl + PAD_B - SUBLANES
    for r in range(1, SUBLANES):
        cs_buf[r - 1] = glu_buf[r:r + ext, :]

    ng = tl // SUBLANES
    a3, b3 = _group_scan(a.reshape(ng, SUBLANES, W_A), b.reshape(ng, SUBLANES, W_A))
    carry = hc_ref[...]
    hs = []
    for g in range(ng):
        hg = b3[g] + a3[g] * carry
        carry = hg[SUBLANES - 1:SUBLANES, :]
        hs.append(hg)
    h = jnp.concatenate(hs, axis=0)
    hc_ref[...] = carry
    hlast_ref[0] = carry
    ga = _dot(xnb, _win(win_ref, O_GA, W_A))
    y_buf[:, 0:W_A] = (h * _silu(ga)).astype(BF16)

    gb = _dot(xnb, _win(win_ref, O_GB, W_B))
    for rb in range(tl // ROW_BLK):
        r0 = rb * ROW_BLK
        cb = cbb_ref[...]
        for k in range(CONV_B):
            q, r = divmod(k + PAD_B - HIST_B, SUBLANES)
            o = r0 + q * SUBLANES
            src = glu_buf[o:o + ROW_BLK, :] if r == 0 else cs_buf[r - 1, o:o + ROW_BLK, :]
            cb = cb + cbw_ref[k:k + 1, :] * src
        yb = _silu(_layernorm(cb, lnbg_ref[...], lnbb_ref[...])) * _silu(gb[r0:r0 + ROW_BLK, :])
        y_buf[r0:r0 + ROW_BLK, W_A:W_A + W_B] = yb.astype(BF16)
    new_hist_b = glu_buf[PAD_B + tl - HIST_B:PAD_B + tl, :]
    histb_ref[0] = new_hist_b
    glu_buf[PAD_B - HIST_B:PAD_B, :] = new_hist_b

    v = _dot(xnb, _win(win_ref, O_V, W_C))
    vnb = _layernorm(v, lncg_ref[...], lncb_ref[...]).astype(BF16)
    tri = _tril_mask(CHUNK)
    lane = lax.broadcasted_iota(jnp.int32, (CHUNK, LANES), 1)
    lo_half = lane < HD_C
    wpair = []
    for p in range(H_C // 2):
        w0 = jnp.where(tri, ws_ref[2 * p], 0.0).astype(BF16)
        w1 = jnp.where(tri, ws_ref[2 * p + 1], 0.0).astype(BF16)
        wpair.append(jnp.concatenate([w0, w1], axis=1))
    zero = jnp.zeros((CHUNK, LANES), BF16)
    chunks = []
    for cc in range(tl // CHUNK):
        cols = []
        for p in range(H_C // 2):
            vp = vnb[cc * CHUNK:(cc + 1) * CHUNK, p * LANES:(p + 1) * LANES]
            rhs = jnp.concatenate([jnp.where(lo_half, vp, zero), jnp.where(lo_half, zero, vp)], axis=0)
            cols.append(_dot(wpair[p], rhs))
        chunks.append(jnp.concatenate(cols, axis=1) + bs_ref[...])
    mixed = jnp.concatenate(chunks, axis=0)
    u = _dot(xnb, _win(win_ref, O_U, W_C))
    gc = _dot(xnb, _win(win_ref, O_GC, W_C))
    y_buf[:, W_A + W_B:MIX] = (u * mixed * _silu(gc)).astype(BF16)

    out = _dot(y_buf[...], wout_ref[...])
    xnew = x + gate * out
    if final:
        xnew = _rmsnorm(xnew, fg_ref[...])
    xo_ref[0] = xnew


def _layer_spec(l, shape):
    nd = len(shape)
    return pl.BlockSpec((None,) + tuple(shape), lambda *_: (l,) + (0,) * nd, pipeline_mode=pl.Buffered(1))


def _const_spec(shape):
    nd = len(shape)
    return pl.BlockSpec(tuple(shape), lambda *_: (0,) * nd, pipeline_mode=pl.Buffered(1))


def _layer_param_specs(l):
    return [
        _layer_spec(l, (1, D_MODEL)),
        pl.BlockSpec(memory_space=pl.ANY),
        pl.BlockSpec(memory_space=pl.ANY),
        _layer_spec(l, (CONV_A, W_A)),
        _layer_spec(l, (1, W_A)),
        _layer_spec(l, (N_GATE_BLK, GATE_BLK, 2 * GATE_BLK)),
        _layer_spec(l, (1, W_A)),
        _layer_spec(l, (1, W_A)),
        _layer_spec(l, (1, W_A)),
        _layer_spec(l, (1, W_B)),
        _layer_spec(l, (1, W_B)),
        _layer_spec(l, (1, W_B)),
        _layer_spec(l, (1, W_C)),
        _layer_spec(l, (1, W_C)),
        _const_spec((1, D_MODEL)),
    ]


def _prompt_layer(l, x, mod_p, p):
    bp, seq, _ = x.shape
    tl = PROMPT_ROWS
    kern = functools.partial(_prompt_kernel, l=l, tl=tl, final=(l == DEPTH - 1))
    in_specs = [
        pl.BlockSpec((1, tl, D_MODEL), lambda b, c: (b, c, 0)),
        pl.BlockSpec((None, 1, 1, 3 * D_MODEL), lambda b, c: (l, b, 0, 0)),
    ] + _layer_param_specs(l) + [
        _layer_spec(l, (CONV_B, W_B)),
        _layer_spec(l, (H_C, CHUNK, CHUNK)),
        _layer_spec(l, (CHUNK, W_C)),
    ]
    out_specs = [
        pl.BlockSpec((1, tl, D_MODEL), lambda b, c: (b, c, 0)),
        pl.BlockSpec((1, HIST_A, W_A), lambda b, c: (b, 0, 0)),
        pl.BlockSpec((1, 1, W_A), lambda b, c: (b, 0, 0)),
        pl.BlockSpec((1, HIST_B, W_B), lambda b, c: (b, 0, 0)),
        pl.BlockSpec(memory_space=pl.ANY),
        pl.BlockSpec(memory_space=pl.ANY),
    ]
    out_shape = [
        jax.ShapeDtypeStruct((bp, seq, D_MODEL), F32),
        jax.ShapeDtypeStruct((bp, HIST_A, W_A), F32),
        jax.ShapeDtypeStruct((bp, 1, W_A), F32),
        jax.ShapeDtypeStruct((bp, HIST_B, W_B), F32),
        jax.ShapeDtypeStruct((D_MODEL, IN_COLS), BF16),
        jax.ShapeDtypeStruct((MIX, D_MODEL), BF16),
    ]
    return pl.pallas_call(
        kern,
        grid=(bp, seq // tl),
        in_specs=in_specs,
        out_specs=out_specs,
        out_shape=out_shape,
        scratch_shapes=[
            pltpu.VMEM((D_MODEL, IN_COLS), BF16),
            pltpu.VMEM((MIX, D_MODEL), BF16),
            pltpu.VMEM((N_STAGE, SLAB_ROWS, IN_COLS), F32),
            pltpu.SemaphoreType.DMA((N_STAGE,)),
            pltpu.SemaphoreType.DMA((2,)),
            pltpu.VMEM((PAD_A + tl, W_A), F32),
            pltpu.VMEM((PAD_B + tl, W_B), F32),
            pltpu.VMEM((SUBLANES - 1, tl + PAD_B - SUBLANES, W_B), F32),
            pltpu.VMEM((1, W_A), F32),
            pltpu.VMEM((tl, MIX), BF16),
        ],
        compiler_params=pltpu.CompilerParams(
            dimension_semantics=("arbitrary", "arbitrary"), vmem_limit_bytes=VMEM_LIMIT),
        name="prompt_layer",
    )(x, mod_p.reshape(DEPTH, bp, 1, 3 * D_MODEL), *p["common"], p["conv_b_w"], p["ws"], p["bs_full"])


def _rows_to_groups(get_row, n, width):
    rows = [jnp.broadcast_to(get_row(i), (SUBLANES, width)) for i in range(n)]
    return jnp.concatenate(rows, axis=0).reshape(n, SUBLANES, width)


def _rows_to_tokens(ref, nb, lo, hi):
    return _rows_to_groups(lambda i: ref[i:i + 1, lo:hi], nb, hi - lo).reshape(nb * SUBLANES, hi - lo)


N_SAMPLE_INPUTS = 24


def _sample_kernel(*refs, nb, final, n_alias):
    (x_ref, mod_ref, ng_ref, win_hbm, wout_hbm, caw_ref, cab_ref, wg_ref, ba_ref, bi_ref,
     lam_ref, cbb_ref, lnbg_ref, lnbb_ref, lncg_ref, lncb_ref, fg_ref,
     wta_ref, wtb_ref, wcol_ref, bs8_ref, hista_in, h0_in, histb_in) = refs[:N_SAMPLE_INPUTS]
    (xo_ref, hista_out, hlast_out, histb_out, vn_out,
     win_ref, wout_ref, sem, rows_buf, glu_rows, y_buf) = refs[N_SAMPLE_INPUTS + n_alias:]

    @pl.when(pl.program_id(0) == 0)
    def _():
        copies = _handoff_copies(win_hbm, wout_hbm, win_ref, wout_ref, sem)
        for cp in copies:
            cp.start()
        for cp in copies:
            cp.wait()

    t = nb * SUBLANES
    x = x_ref[...].reshape(t, D_MODEL)
    shift = _rows_to_tokens(mod_ref, nb, 0, D_MODEL)
    scale = _rows_to_tokens(mod_ref, nb, D_MODEL, 2 * D_MODEL)
    xnb = _norm_mod(x, ng_ref[...], scale, shift).astype(BF16)

    xa3 = _dot(xnb, _win(win_ref, O_XA, W_A)).reshape(nb, SUBLANES, W_A)
    rows_buf[...] = xa3
    row = lax.broadcasted_iota(jnp.int32, (1, SUBLANES, W_A), 1)
    xac3 = cab_ref[...] + caw_ref[HIST_A:HIST_A + 1, :] * xa3
    for s in range(1, CONV_A):
        shifted = jnp.where(row >= s, pltpu.roll(xa3, s, axis=1), 0.0)
        xac3 = xac3 + caw_ref[HIST_A - s:HIST_A - s + 1, :] * shifted
    for j in range(HIST_A):
        hist3 = _rows_to_groups(lambda i, j=j: hista_in[j, i:i + 1, :], nb, W_A)
        xac3 = xac3 + wta_ref[j] * hist3
    for k in range(HIST_A):
        hista_out[k] = rows_buf[:, SUBLANES - HIST_A + k, :]
    a, b = _lru_coeffs(xac3.reshape(t, W_A), wg_ref, ba_ref[...], bi_ref[...], lam_ref[...])
    a3, b3 = _group_scan(a.reshape(nb, SUBLANES, W_A), b.reshape(nb, SUBLANES, W_A))
    h3 = b3 + a3 * _rows_to_groups(lambda i: h0_in[i:i + 1, :], nb, W_A)
    rows_buf[...] = h3
    hlast_out[...] = rows_buf[:, SUBLANES - 1, :]
    ga = _dot(xnb, _win(win_ref, O_GA, W_A))
    y_buf[:, 0:W_A] = (h3.reshape(t, W_A) * _silu(ga)).astype(BF16)

    gla = _dot(xnb, _win(win_ref, O_GLA, W_B))
    glb = _dot(xnb, _win(win_ref, O_GLB, W_B))
    glu3 = (gla * _sigmoid(glb)).reshape(nb, SUBLANES, W_B)
    glu_rows[...] = glu3
    gb = _dot(xnb, _win(win_ref, O_GB, W_B))
    for sb in range(nb // SEQ_BLK):
        s0 = sb * SEQ_BLK
        cb3 = cbb_ref[...]
        for j in range(HIST_B + SUBLANES):
            if j < HIST_B:
                rowj = _rows_to_groups(lambda i, j=j: histb_in[j, s0 + i:s0 + i + 1, :], SEQ_BLK, W_B)
            else:
                rowj = glu3[s0:s0 + SEQ_BLK, j - HIST_B:j - HIST_B + 1, :]
            cb3 = cb3 + wtb_ref[j] * rowj
        cb = cb3.reshape(SEQ_BLK * SUBLANES, W_B)
        r0 = s0 * SUBLANES
        yb = _silu(_layernorm(cb, lnbg_ref[...], lnbb_ref[...])) * _silu(gb[r0:r0 + SEQ_BLK * SUBLANES, :])
        y_buf[r0:r0 + SEQ_BLK * SUBLANES, W_A:W_A + W_B] = yb.astype(BF16)
    histb_out[0:HIST_B - SUBLANES] = histb_in[SUBLANES:HIST_B]
    for s in range(SUBLANES):
        histb_out[HIST_B - SUBLANES + s] = glu_rows[:, s, :]

    v = _dot(xnb, _win(win_ref, O_V, W_C))
    vn3 = _layernorm(v, lncg_ref[...], lncb_ref[...]).reshape(nb, SUBLANES, W_C)
    vn_out[...] = vn3
    trow = lax.broadcasted_iota(jnp.int32, (SUBLANES, W_C), 0)
    mixed3 = jnp.broadcast_to(bs8_ref[...], (nb, SUBLANES, W_C))
    for s in range(SUBLANES):
        wc = jnp.where(trow >= s, wcol_ref[s], 0.0)
        mixed3 = mixed3 + wc * vn3[:, s:s + 1, :]
    u = _dot(xnb, _win(win_ref, O_U, W_C))
    gc = _dot(xnb, _win(win_ref, O_GC, W_C))
    y_buf[:, W_A + W_B:MIX] = (u * mixed3.reshape(t, W_C) * _silu(gc)).astype(BF16)

    out = _dot(y_buf[...], wout_ref[...])
    gate = _rows_to_tokens(mod_ref, nb, 2 * D_MODEL, 3 * D_MODEL)
    xnew = x + gate * out
    if final:
        xnew = _rmsnorm(xnew, fg_ref[...])
    xo_ref[...] = xnew.reshape(nb, SUBLANES, D_MODEL)


def _sample_layer(l, x, mod_s, hist_a, h0, hist_b, p, wbf, prev):
    bs, ls, _ = x.shape
    assert ls == SUBLANES
    nb = SAMPLE_SEQS
    out_shape = [
        jax.ShapeDtypeStruct((bs, SUBLANES, D_MODEL), F32),
        jax.ShapeDtypeStruct((DEPTH, HIST_A, bs, W_A), F32),
        jax.ShapeDtypeStruct((DEPTH, bs, W_A), F32),
        jax.ShapeDtypeStruct((DEPTH, HIST_B, bs, W_B), F32),
        jax.ShapeDtypeStruct((DEPTH, bs, SUBLANES, W_C), F32),
    ]
    if prev is None:
        prev = tuple(jnp.zeros(s.shape, s.dtype) for s in out_shape[1:])
    n_alias = len(prev)
    kern = functools.partial(_sample_kernel, nb=nb, final=(l == DEPTH - 1), n_alias=n_alias)
    in_specs = [
        pl.BlockSpec((nb, SUBLANES, D_MODEL), lambda i: (i, 0, 0)),
        pl.BlockSpec((None, nb, 3 * D_MODEL), lambda i: (l, i, 0)),
    ] + _layer_param_specs(l) + [
        _layer_spec(l, (HIST_A, SUBLANES, W_A)),
        _layer_spec(l, (HIST_B + SUBLANES, SUBLANES, W_B)),
        _layer_spec(l, (SUBLANES, SUBLANES, W_C)),
        _layer_spec(l, (SUBLANES, W_C)),
        pl.BlockSpec((None, HIST_A, nb, W_A), lambda i: (l, 0, i, 0)),
        pl.BlockSpec((None, nb, W_A), lambda i: (l, i, 0)),
        pl.BlockSpec((None, HIST_B, nb, W_B), lambda i: (l, 0, i, 0)),
    ] + [pl.BlockSpec(memory_space=pl.ANY)] * n_alias
    out_specs = [
        pl.BlockSpec((nb, SUBLANES, D_MODEL), lambda i: (i, 0, 0)),
        pl.BlockSpec((None, HIST_A, nb, W_A), lambda i: (l, 0, i, 0)),
        pl.BlockSpec((None, nb, W_A), lambda i: (l, i, 0)),
        pl.BlockSpec((None, HIST_B, nb, W_B), lambda i: (l, 0, i, 0)),
        pl.BlockSpec((None, nb, SUBLANES, W_C), lambda i: (l, i, 0, 0)),
    ]
    common = list(p["common"])
    common[1:3] = wbf
    args = [x, mod_s, *common, p["wta"], p["wtb"], p["wcol"], p["bs8"], hist_a, h0, hist_b]
    assert len(args) == N_SAMPLE_INPUTS
    aliases = {len(args) + k: 1 + k for k in range(n_alias)}
    args += list(prev)
    outs = pl.pallas_call(
        kern,
        grid=(bs // nb,),
        in_specs=in_specs,
        out_specs=out_specs,
        out_shape=out_shape,
        input_output_aliases=aliases,
        scratch_shapes=[
            pltpu.VMEM((D_MODEL, IN_COLS), BF16),
            pltpu.VMEM((MIX, D_MODEL), BF16),
            pltpu.SemaphoreType.DMA((2,)),
            pltpu.VMEM((nb, SUBLANES, W_A), F32),
            pltpu.VMEM((nb, SUBLANES, W_B), F32),
            pltpu.VMEM((nb * SUBLANES, MIX), BF16),
        ],
        compiler_params=pltpu.CompilerParams(
            dimension_semantics=("arbitrary",), vmem_limit_bytes=VMEM_LIMIT),
        name="sample_layer",
    )(*args)
    return outs[0], tuple(outs[1:])


def _block_diag_gates(wa, wi):
    per = GATE_BLK // HD_A
    r = jnp.arange(GATE_BLK)[:, None] // HD_A
    c = jnp.arange(GATE_BLK)[None, :] // HD_A

    def bd(w):
        rows = w.reshape(DEPTH, N_GATE_BLK, GATE_BLK, HD_A)
        return jnp.where(r == c, jnp.tile(rows, (1, 1, 1, per)), 0.0)

    return jnp.concatenate([bd(wa), bd(wi)], axis=-1)


def _toeplitz_tiles(w, n_rows):
    taps = w.shape[1]
    j = jnp.arange(n_rows)[:, None]
    t = jnp.arange(SUBLANES)[None, :]
    idx = j - t
    valid = (idx >= 0) & (idx < taps)
    tiles = w[:, jnp.clip(idx, 0, taps - 1), :]
    return jnp.where(valid[None, :, :, None], tiles, 0.0)


def _prep_params(norm_g, w_in, conv_a_w, conv_a_b, lru_wa, lru_ba, lru_wi, lru_bi, lru_lam, conv_b_w,
                 conv_b_b, ln_b_g, ln_b_b, ln_c_g, ln_c_b, gmlp_ws, gmlp_bs, w_out, final_g):
    row = lambda v: v.reshape(DEPTH, 1, -1)
    common = (
        row(norm_g), w_in, w_out, conv_a_w, row(conv_a_b),
        _block_diag_gates(lru_wa, lru_wi).astype(BF16), row(lru_ba), row(lru_bi), row(lru_lam),
        row(conv_b_b), row(ln_b_g), row(ln_b_b), row(ln_c_g), row(ln_c_b),
        final_g.reshape(1, D_MODEL),
    )
    bs_full = jnp.repeat(jnp.transpose(gmlp_bs, (0, 2, 1)), HD_C, axis=2)
    wcol = jnp.repeat(jnp.transpose(gmlp_ws[:, :, :SUBLANES, :SUBLANES], (0, 3, 2, 1)), HD_C, axis=3)
    return {"common": common, "conv_b_w": conv_b_w, "ws": gmlp_ws, "bs_full": bs_full, "wcol": wcol,
            "bs8": bs_full[:, :SUBLANES], "wta": _toeplitz_tiles(conv_a_w, HIST_A),
            "wtb": _toeplitz_tiles(conv_b_w, HIST_B + SUBLANES)}


def kernel(x_prompt, x_sample, c_prompt, c_sample, state_lru_conv, state_lru_h, state_ccm_conv, norm_g, w_ada,
           b_ada, w_in, conv_a_w, conv_a_b, lru_wa, lru_ba, lru_wi, lru_bi, lru_lam, conv_b_w, conv_b_b, ln_b_g,
           ln_b_b, ln_c_g, ln_c_b, gmlp_ws, gmlp_bs, w_out, final_g):
    bp = x_prompt.shape[0]
    mod_p, mod_s = _ada_call(c_prompt, c_sample, w_ada, b_ada)
    p = _prep_params(norm_g, w_in, conv_a_w, conv_a_b, lru_wa, lru_ba, lru_wi, lru_bi, lru_lam, conv_b_w,
                     conv_b_b, ln_b_g, ln_b_b, ln_c_g, ln_c_b, gmlp_ws, gmlp_bs, w_out, final_g)
    hist_a_s = jnp.transpose(state_lru_conv, (0, 2, 1, 3))
    hist_b_s = jnp.transpose(state_ccm_conv, (0, 2, 1, 3))
    xp, xs = x_prompt, x_sample
    conv_a_p, h_p, conv_b_p = [], [], []
    sample_state = None
    for l in range(DEPTH):
        xp, ha, hl, hb, win_bf, wout_bf = _prompt_layer(l, xp, mod_p, p)
        conv_a_p.append(ha); h_p.append(hl.reshape(bp, W_A)); conv_b_p.append(hb)
        xs, sample_state = _sample_layer(l, xs, mod_s, hist_a_s, state_lru_h, hist_b_s, p,
                                         (win_bf, wout_bf), sample_state)
    new_a_s, new_h_s, new_b_s, new_v_s = sample_state
    return (xp, xs,
            jnp.stack(conv_a_p), jnp.stack(h_p), jnp.stack(conv_b_p),
            jnp.transpose(new_a_s, (0, 2, 1, 3)), new_h_s, jnp.transpose(new_b_s, (0, 2, 1, 3)), new_v_s)
```

```python
import functools

import jax
import jax.numpy as jnp
from jax import lax
from jax.experimental import pallas as pl
from jax.experimental.pallas import tpu as pltpu

F32 = jnp.float32
BF16 = jnp.bfloat16

D_MODEL = 2048
DEPTH = 4
MIX = D_MODEL
W_A = MIX // 2
H_A = 16
HD_A = W_A // H_A
CONV_A = 4
C_LRU = 8.0
W_B = MIX // 4
CONV_B = 31
W_C = MIX // 4
H_C = 8
HD_C = W_C // H_C
CHUNK = 128
EPS = 1e-6
IN_COLS = 2 * W_A + 3 * W_B + 3 * W_C

O_XA = 0
O_GA = O_XA + W_A
O_GLA = O_GA + W_A
O_GLB = O_GLA + W_B
O_GB = O_GLB + W_B
O_U = O_GB + W_B
O_V = O_U + W_C
O_GC = O_V + W_C

SUBLANES = 8
LANES = 128
GATE_BLK = 256
N_GATE_BLK = W_A // GATE_BLK
HIST_A = CONV_A - 1
HIST_B = CONV_B - 1
PAD_A = SUBLANES
PAD_B = 32
SLAB_ROWS = 64
N_STAGE = 8
ROW_BLK = 64

PROMPT_ROWS = 256
SAMPLE_SEQS = 16
ADA_COLS = 1024
VMEM_LIMIT = 58 * 1024 * 1024


def _dot(a, b):
    return jnp.dot(a, b, preferred_element_type=F32)


def _slab_copy(w_hbm, l, s, width, stage, sem):
    slot = s % N_STAGE
    return pltpu.make_async_copy(
        w_hbm.at[l, pl.ds(s * SLAB_ROWS, SLAB_ROWS), :], stage.at[slot, :, 0:width], sem.at[slot])


def _load_cast_weights(w_hbm, l, dst, stage, sem):
    rows, width = dst.shape
    n_slabs = rows // SLAB_ROWS
    ahead = N_STAGE - 1
    assert n_slabs > ahead
    for s in range(ahead):
        _slab_copy(w_hbm, l, s, width, stage, sem).start()

    def body(s, carry):
        @pl.when(s + ahead < n_slabs)
        def _():
            _slab_copy(w_hbm, l, s + ahead, width, stage, sem).start()

        _slab_copy(w_hbm, l, s, width, stage, sem).wait()
        r0 = pl.multiple_of(s * SLAB_ROWS, SLAB_ROWS)
        dst[pl.ds(r0, SLAB_ROWS), :] = stage[s % N_STAGE, :, 0:width].astype(BF16)
        return carry

    lax.fori_loop(0, n_slabs, body, 0)


def _load_layer_weights(l, win_hbm, wout_hbm, win_s, wout_s, stage, sem):
    _load_cast_weights(win_hbm, l, win_s, stage, sem)
    _load_cast_weights(wout_hbm, l, wout_s, stage, sem)


def _handoff_copies(win_src, wout_src, win_dst, wout_dst, sem):
    return (pltpu.make_async_copy(win_src, win_dst, sem.at[0]),
            pltpu.make_async_copy(wout_src, wout_dst, sem.at[1]))


def _sigmoid(x):
    return 0.5 * jnp.tanh(0.5 * x) + 0.5


def _silu(x):
    return x * _sigmoid(x)


def _layernorm(x, g, b):
    mu = jnp.mean(x, axis=-1, keepdims=True)
    xc = x - mu
    return xc * lax.rsqrt(jnp.mean(xc * xc, axis=-1, keepdims=True) + EPS) * g + b


def _rmsnorm(x, g):
    return (x * lax.rsqrt(jnp.mean(x * x, axis=-1, keepdims=True) + EPS)) * g


def _norm_mod(x, g, scale, shift):
    return _rmsnorm(x, g) * (1.0 + scale) + shift


def _lru_coeffs(xac, wg_ref, ba, bi, lam):
    xacb = xac.astype(BF16)
    pa, pi = [], []
    for j in range(N_GATE_BLK):
        g = _dot(xacb[:, j * GATE_BLK:(j + 1) * GATE_BLK], wg_ref[j])
        pa.append(g[:, :GATE_BLK])
        pi.append(g[:, GATE_BLK:])
    r = _sigmoid(jnp.concatenate(pa, axis=1) + ba)
    i = _sigmoid(jnp.concatenate(pi, axis=1) + bi)
    log_a = (-C_LRU * jax.nn.softplus(-lam)) * r
    a = jnp.exp(log_a)
    coef = jnp.sqrt(-jnp.tanh(log_a) * (1.0 + a * a))
    return a, coef * (i * xac)


def _group_scan(a3, b3):
    row = lax.broadcasted_iota(jnp.int32, (1,) + a3.shape[1:], 1)
    for s in (1, 2, 4):
        keep = row >= s
        a_prev = jnp.where(keep, pltpu.roll(a3, s, axis=1), 1.0)
        b_prev = jnp.where(keep, pltpu.roll(b3, s, axis=1), 0.0)
        b3 = a3 * b_prev + b3
        a3 = a3 * a_prev
    return a3, b3


def _tril_mask(n):
    r = lax.broadcasted_iota(jnp.int32, (n, n), 0)
    c = lax.broadcasted_iota(jnp.int32, (n, n), 1)
    return c <= r


def _win(win_ref, off, width):
    return win_ref[:, off:off + width]


def _ada_kernel(cp_ref, cs_ref, w_ref, b_ref, mp_ref, ms_ref):
    w = w_ref[0].astype(BF16)
    b = b_ref[0]
    mp_ref[0] = _dot(_silu(cp_ref[...]).astype(BF16), w) + b
    ms_ref[0] = _dot(_silu(cs_ref[...]).astype(BF16), w) + b


def _ada_call(c_prompt, c_sample, w_ada, b_ada):
    bp, bs = c_prompt.shape[0], c_sample.shape[0]
    n3 = w_ada.shape[-1]
    tn = ADA_COLS
    return pl.pallas_call(
        _ada_kernel,
        grid=(DEPTH, n3 // tn),
        in_specs=[
            pl.BlockSpec((bp, D_MODEL), lambda l, j: (0, 0)),
            pl.BlockSpec((bs, D_MODEL), lambda l, j: (0, 0)),
            pl.BlockSpec((1, D_MODEL, tn), lambda l, j: (l, 0, j)),
            pl.BlockSpec((1, 1, tn), lambda l, j: (l, 0, j)),
        ],
        out_specs=[
            pl.BlockSpec((1, bp, tn), lambda l, j: (l, 0, j)),
            pl.BlockSpec((1, bs, tn), lambda l, j: (l, 0, j)),
        ],
        out_shape=[
            jax.ShapeDtypeStruct((DEPTH, bp, n3), F32),
            jax.ShapeDtypeStruct((DEPTH, bs, n3), F32),
        ],
        compiler_params=pltpu.CompilerParams(
            dimension_semantics=("arbitrary", "arbitrary"), vmem_limit_bytes=VMEM_LIMIT),
        name="ada_mod",
    )(c_prompt, c_sample, w_ada, b_ada.reshape(DEPTH, 1, n3))


def _prompt_kernel(x_ref, mod_ref, ng_ref, win_hbm, wout_hbm, caw_ref, cab_ref, wg_ref, ba_ref, bi_ref,
                   lam_ref, cbb_ref, lnbg_ref, lnbb_ref, lncg_ref, lncb_ref, fg_ref, cbw_ref, ws_ref, bs_ref,
                   xo_ref, hista_ref, hlast_ref, histb_ref, win_bf, wout_bf,
                   win_ref, wout_ref, stage, sem, hsem, xa_buf, glu_buf, cs_buf, hc_ref, y_buf,
                   *, l, tl, final):
    b_idx = pl.program_id(0)
    c = pl.program_id(1)

    @pl.when((b_idx == 0) & (c == 0))
    def _():
        _load_layer_weights(l, win_hbm, wout_hbm, win_ref, wout_ref, stage, sem)
        for cp in _handoff_copies(win_ref, wout_ref, win_bf, wout_bf, hsem):
            cp.start()

    @pl.when((b_idx == pl.num_programs(0) - 1) & (c == pl.num_programs(1) - 1))
    def _():
        for cp in _handoff_copies(win_ref, wout_ref, win_bf, wout_bf, hsem):
            cp.wait()

    @pl.when(c == 0)
    def _():
        xa_buf[0:PAD_A, :] = jnp.zeros((PAD_A, W_A), F32)
        glu_buf[0:PAD_B, :] = jnp.zeros((PAD_B, W_B), F32)
        hc_ref[...] = jnp.zeros_like(hc_ref)

    x = x_ref[0]
    shift = mod_ref[0, :, 0:D_MODEL]
    scale = mod_ref[0, :, D_MODEL:2 * D_MODEL]
    gate = mod_ref[0, :, 2 * D_MODEL:3 * D_MODEL]
    xnb = _norm_mod(x, ng_ref[...], scale, shift).astype(BF16)

    xa_buf[PAD_A:PAD_A + tl, :] = _dot(xnb, _win(win_ref, O_XA, W_A))
    xac = cab_ref[...]
    for k in range(CONV_A):
        o = PAD_A - HIST_A + k
        xac = xac + caw_ref[k:k + 1, :] * xa_buf[o:o + tl, :]
    new_hist_a = xa_buf[PAD_A + tl - HIST_A:PAD_A + tl, :]
    hista_ref[0] = new_hist_a
    xa_buf[PAD_A - HIST_A:PAD_A, :] = new_hist_a
    a, b = _lru_coeffs(xac, wg_ref, ba_ref[...], bi_ref[...], lam_ref[...])

    gla = _dot(xnb, _win(win_ref, O_GLA, W_B))
    glb = _dot(xnb, _win(win_ref, O_GLB, W_B))
    glu_buf[PAD_B:PAD_B + tl, :] = gla * _sigmoid(glb)
    ext = tl + PAD_B - SUBLANES
    for r in range(1, SUBLANES):
        cs_buf[r - 1] = glu_buf[r:r + ext, :]

    ng = tl // SUBLANES
    a3, b3 = _group_scan(a.reshape(ng, SUBLANES, W_A), b.reshape(ng, SUBLANES, W_A))
    carry = hc_ref[...]
    hs = []
    for g in range(ng):
        hg = b3[g] + a3[g] * carry
        carry = hg[SUBLANES - 1:SUBLANES, :]
        hs.append(hg)
    h = jnp.concatenate(hs, axis=0)
    hc_ref[...] = carry
    hlast_ref[0] = carry
    ga = _dot(xnb, _win(win_ref, O_GA, W_A))
    y_buf[:, 0:W_A] = (h * _silu(ga)).astype(BF16)

    gb = _dot(xnb, _win(win_ref, O_GB, W_B))
    for rb in range(tl // ROW_BLK):
        r0 = rb * ROW_BLK
        cb = cbb_ref[...]
        for k in range(CONV_B):
            q, r = divmod(k + PAD_B - HIST_B, SUBLANES)
            o = r0 + q * SUBLANES
            src = glu_buf[o:o + ROW_BLK, :] if r == 0 else cs_buf[r - 1, o:o + ROW_BLK, :]
            cb = cb + cbw_ref[k:k + 1, :] * src
        yb = _silu(_layernorm(cb, lnbg_ref[...], lnbb_ref[...])) * _silu(gb[r0:r0 + ROW_BLK, :])
        y_buf[r0:r0 + ROW_BLK, W_A:W_A + W_B] = yb.astype(BF16)
    new_hist_b = glu_buf[PAD_B + tl - HIST_B:PAD_B + tl, :]
    histb_ref[0] = new_hist_b
    glu_buf[PAD_B - HIST_B:PAD_B, :] = new_hist_b

    v = _dot(xnb, _win(win_ref, O_V, W_C))
    vnb = _layernorm(v, lncg_ref[...], lncb_ref[...]).astype(BF16)
    tri = _tril_mask(CHUNK)
    lane = lax.broadcasted_iota(jnp.int32, (CHUNK, LANES), 1)
    lo_half = lane < HD_C
    wpair = []
    for p in range(H_C // 2):
        w0 = jnp.where(tri, ws_ref[2 * p], 0.0).astype(BF16)
        w1 = jnp.where(tri, ws_ref[2 * p + 1], 0.0).astype(BF16)
        wpair.append(jnp.concatenate([w0, w1], axis=1))
    zero = jnp.zeros((CHUNK, LANES), BF16)
    chunks = []
    for cc in range(tl // CHUNK):
        cols = []
        for p in range(H_C // 2):
            vp = vnb[cc * CHUNK:(cc + 1) * CHUNK, p * LANES:(p + 1) * LANES]
            rhs = jnp.concatenate([jnp.where(lo_half, vp, zero), jnp.where(lo_half, zero, vp)], axis=0)
            cols.append(_dot(wpair[p], rhs))
        chunks.append(jnp.concatenate(cols, axis=1) + bs_ref[...])
    mixed = jnp.concatenate(chunks, axis=0)
    u = _dot(xnb, _win(win_ref, O_U, W_C))
    gc = _dot(xnb, _win(win_ref, O_GC, W_C))
    y_buf[:, W_A + W_B:MIX] = (u * mixed * _silu(gc)).astype(BF16)

    out = _dot(y_buf[...], wout_ref[...])
    xnew = x + gate * out
    if final:
        xnew = _rmsnorm(xnew, fg_ref[...])
    xo_ref[0] = xnew


def _layer_spec(l, shape):
    nd = len(shape)
    return pl.BlockSpec((None,) + tuple(shape), lambda *_: (l,) + (0,) * nd, pipeline_mode=pl.Buffered(1))


def _const_spec(shape):
    nd = len(shape)
    return pl.BlockSpec(tuple(shape), lambda *_: (0,) * nd, pipeline_mode=pl.Buffered(1))


def _layer_param_specs(l):
    return [
        _layer_spec(l, (1, D_MODEL)),
        pl.BlockSpec(memory_space=pl.ANY),
        pl.BlockSpec(memory_space=pl.ANY),
        _layer_spec(l, (CONV_A, W_A)),
        _layer_spec(l, (1, W_A)),
        _layer_spec(l, (N_GATE_BLK, GATE_BLK, 2 * GATE_BLK)),
        _layer_spec(l, (1, W_A)),
        _layer_spec(l, (1, W_A)),
        _layer_spec(l, (1, W_A)),
        _layer_spec(l, (1, W_B)),
        _layer_spec(l, (1, W_B)),
        _layer_spec(l, (1, W_B)),
        _layer_spec(l, (1, W_C)),
        _layer_spec(l, (1, W_C)),
        _const_spec((1, D_MODEL)),
    ]


def _prompt_layer(l, x, mod_p, p):
    bp, seq, _ = x.shape
    tl = PROMPT_ROWS
    kern = functools.partial(_prompt_kernel, l=l, tl=tl, final=(l == DEPTH - 1))
    in_specs = [
        pl.BlockSpec((1, tl, D_MODEL), lambda b, c: (b, c, 0)),
        pl.BlockSpec((None, 1, 1, 3 * D_MODEL), lambda b, c: (l, b, 0, 0)),
    ] + _layer_param_specs(l) + [
        _layer_spec(l, (CONV_B, W_B)),
        _layer_spec(l, (H_C, CHUNK, CHUNK)),
        _layer_spec(l, (CHUNK, W_C)),
    ]
    out_specs = [
        pl.BlockSpec((1, tl, D_MODEL), lambda b, c: (b, c, 0)),
        pl.BlockSpec((1, HIST_A, W_A), lambda b, c: (b, 0, 0)),
        pl.BlockSpec((1, 1, W_A), lambda b, c: (b, 0, 0)),
        pl.BlockSpec((1, HIST_B, W_B), lambda b, c: (b, 0, 0)),
        pl.BlockSpec(memory_space=pl.ANY),
        pl.BlockSpec(memory_space=pl.ANY),
    ]
    out_shape = [
        jax.ShapeDtypeStruct((bp, seq, D_MODEL), F32),
        jax.ShapeDtypeStruct((bp, HIST_A, W_A), F32),
        jax.ShapeDtypeStruct((bp, 1, W_A), F32),
        jax.ShapeDtypeStruct((bp, HIST_B, W_B), F32),
        jax.ShapeDtypeStruct((D_MODEL, IN_COLS), BF16),
        jax.ShapeDtypeStruct((MIX, D_MODEL), BF16),
    ]
    return pl.pallas_call(
        kern,
        grid=(bp, seq // tl),
        in_specs=in_specs,
        out_specs=out_specs,
        out_shape=out_shape,
        scratch_shapes=[
            pltpu.VMEM((D_MODEL, IN_COLS), BF16),
            pltpu.VMEM((MIX, D_MODEL), BF16),
            pltpu.VMEM((N_STAGE, SLAB_ROWS, IN_COLS), F32),
            pltpu.SemaphoreType.DMA((N_STAGE,)),
            pltpu.SemaphoreType.DMA((2,)),
            pltpu.VMEM((PAD_A + tl, W_A), F32),
            pltpu.VMEM((PAD_B + tl, W_B), F32),
            pltpu.VMEM((SUBLANES - 1, tl + PAD_B - SUBLANES, W_B), F32),
            pltpu.VMEM((1, W_A), F32),
            pltpu.VMEM((tl, MIX), BF16),
        ],
        compiler_params=pltpu.CompilerParams(
            dimension_semantics=("arbitrary", "arbitrary"), vmem_limit_bytes=VMEM_LIMIT),
        name="prompt_layer",
    )(x, mod_p.reshape(DEPTH, bp, 1, 3 * D_MODEL), *p["common"], p["conv_b_w"], p["ws"], p["bs_full"])


N_SAMPLE_INPUTS = 23


def _sample_kernel(*refs, nb, final, n_alias):
    (x_ref, mod_ref, ng_ref, win_hbm, wout_hbm, caw_ref, cab_ref, wg_ref, ba_ref, bi_ref,
     lam_ref, cbb_ref, lnbg_ref, lnbb_ref, lncg_ref, lncb_ref, fg_ref,
     cbw_ref, wcol_ref, bs8_ref, hista_in, h0_in, histb_in) = refs[:N_SAMPLE_INPUTS]
    (xo_ref, hista_out, hlast_out, histb_out, vn_out,
     win_ref, wout_ref, sem, y_buf) = refs[N_SAMPLE_INPUTS + n_alias:]

    @pl.when(pl.program_id(0) == 0)
    def _():
        copies = _handoff_copies(win_hbm, wout_hbm, win_ref, wout_ref, sem)
        for cp in copies:
            cp.start()
        for cp in copies:
            cp.wait()

    nt = SUBLANES

    def blk(v, s):
        return v[s * nb:(s + 1) * nb, :]

    def per_token(v):
        return jnp.concatenate([v] * nt, axis=0)

    x = jnp.concatenate([x_ref[:, s, :] for s in range(nt)], axis=0)
    shift = per_token(mod_ref[:, 0:D_MODEL])
    scale = per_token(mod_ref[:, D_MODEL:2 * D_MODEL])
    xnb = _norm_mod(x, ng_ref[...], scale, shift).astype(BF16)

    xa = _dot(xnb, _win(win_ref, O_XA, W_A))
    xc = [hista_in[j] for j in range(HIST_A)] + [blk(xa, s) for s in range(nt)]
    xac = []
    for s in range(nt):
        acc = cab_ref[...]
        for k in range(CONV_A):
            acc = acc + caw_ref[k:k + 1, :] * xc[s + k]
        xac.append(acc)
    for k in range(HIST_A):
        hista_out[k] = xc[nt + k]
    a, b = _lru_coeffs(jnp.concatenate(xac, axis=0), wg_ref, ba_ref[...], bi_ref[...], lam_ref[...])
    h = h0_in[...]
    hs = []
    for s in range(nt):
        h = blk(a, s) * h + blk(b, s)
        hs.append(h)
    hlast_out[...] = h
    ga = _dot(xnb, _win(win_ref, O_GA, W_A))
    y_buf[:, 0:W_A] = (jnp.concatenate(hs, axis=0) * _silu(ga)).astype(BF16)

    gla = _dot(xnb, _win(win_ref, O_GLA, W_B))
    glb = _dot(xnb, _win(win_ref, O_GLB, W_B))
    glu = gla * _sigmoid(glb)
    xcb = [histb_in[j] for j in range(HIST_B)] + [blk(glu, s) for s in range(nt)]
    cb = []
    for s in range(nt):
        acc = cbb_ref[...]
        for k in range(CONV_B):
            acc = acc + cbw_ref[k:k + 1, :] * xcb[s + k]
        cb.append(acc)
    histb_out[0:HIST_B - nt] = histb_in[nt:HIST_B]
    for s in range(nt):
        histb_out[HIST_B - nt + s] = blk(glu, s)
    gb = _dot(xnb, _win(win_ref, O_GB, W_B))
    yb = _silu(_layernorm(jnp.concatenate(cb, axis=0), lnbg_ref[...], lnbb_ref[...])) * _silu(gb)
    y_buf[:, W_A:W_A + W_B] = yb.astype(BF16)

    v = _dot(xnb, _win(win_ref, O_V, W_C))
    vn = _layernorm(v, lncg_ref[...], lncb_ref[...])
    for s in range(nt):
        vn_out[:, s, :] = blk(vn, s)
    mixed = []
    for tt in range(nt):
        acc = bs8_ref[tt:tt + 1, :]
        for s in range(tt + 1):
            acc = acc + wcol_ref[s, tt:tt + 1, :] * blk(vn, s)
        mixed.append(acc)
    u = _dot(xnb, _win(win_ref, O_U, W_C))
    gc = _dot(xnb, _win(win_ref, O_GC, W_C))
    y_buf[:, W_A + W_B:MIX] = (u * jnp.concatenate(mixed, axis=0) * _silu(gc)).astype(BF16)

    out = _dot(y_buf[...], wout_ref[...])
    xnew = x + per_token(mod_ref[:, 2 * D_MODEL:3 * D_MODEL]) * out
    if final:
        xnew = _rmsnorm(xnew, fg_ref[...])
    for s in range(nt):
        xo_ref[:, s, :] = blk(xnew, s)


def _sample_layer(l, x, mod_s, hist_a, h0, hist_b, p, wbf, prev):
    bs, ls, _ = x.shape
    assert ls == SUBLANES
    nb = SAMPLE_SEQS
    out_shape = [
        jax.ShapeDtypeStruct((bs, SUBLANES, D_MODEL), F32),
        jax.ShapeDtypeStruct((DEPTH, HIST_A, bs, W_A), F32),
        jax.ShapeDtypeStruct((DEPTH, bs, W_A), F32),
        jax.ShapeDtypeStruct((DEPTH, HIST_B, bs, W_B), F32),
        jax.ShapeDtypeStruct((DEPTH, bs, SUBLANES, W_C), F32),
    ]
    if prev is None:
        prev = tuple(jnp.zeros(s.shape, s.dtype) for s in out_shape[1:])
    n_alias = len(prev)
    kern = functools.partial(_sample_kernel, nb=nb, final=(l == DEPTH - 1), n_alias=n_alias)
    in_specs = [
        pl.BlockSpec((nb, SUBLANES, D_MODEL), lambda i: (i, 0, 0)),
        pl.BlockSpec((None, nb, 3 * D_MODEL), lambda i: (l, i, 0)),
    ] + _layer_param_specs(l) + [
        _layer_spec(l, (CONV_B, W_B)),
        _layer_spec(l, (SUBLANES, SUBLANES, W_C)),
        _layer_spec(l, (SUBLANES, W_C)),
        pl.BlockSpec((None, HIST_A, nb, W_A), lambda i: (l, 0, i, 0)),
        pl.BlockSpec((None, nb, W_A), lambda i: (l, i, 0)),
        pl.BlockSpec((None, HIST_B, nb, W_B), lambda i: (l, 0, i, 0)),
    ] + [pl.BlockSpec(memory_space=pl.ANY)] * n_alias
    out_specs = [
        pl.BlockSpec((nb, SUBLANES, D_MODEL), lambda i: (i, 0, 0)),
        pl.BlockSpec((None, HIST_A, nb, W_A), lambda i: (l, 0, i, 0)),
        pl.BlockSpec((None, nb, W_A), lambda i: (l, i, 0)),
        pl.BlockSpec((None, HIST_B, nb, W_B), lambda i: (l, 0, i, 0)),
        pl.BlockSpec((None, nb, SUBLANES, W_C), lambda i: (l, i, 0, 0)),
    ]
    common = list(p["common"])
    common[1:3] = wbf
    args = [x, mod_s, *common, p["conv_b_w"], p["wcol"], p["bs8"], hist_a, h0, hist_b]
    assert len(args) == N_SAMPLE_INPUTS
    aliases = {len(args) + k: 1 + k for k in range(n_alias)}
    args += list(prev)
    outs = pl.pallas_call(
        kern,
        grid=(bs // nb,),
        in_specs=in_specs,
        out_specs=out_specs,
        out_shape=out_shape,
        input_output_aliases=aliases,
        scratch_shapes=[
            pltpu.VMEM((D_MODEL, IN_COLS), BF16),
            pltpu.VMEM((MIX, D_MODEL), BF16),
            pltpu.SemaphoreType.DMA((2,)),
            pltpu.VMEM((nb * SUBLANES, MIX), BF16),
        ],
        compiler_params=pltpu.CompilerParams(
            dimension_semantics=("arbitrary",), vmem_limit_bytes=VMEM_LIMIT),
        name="sample_layer",
    )(*args)
    return outs[0], tuple(outs[1:])


def _block_diag_gates(wa, wi):
    per = GATE_BLK // HD_A
    r = jnp.arange(GATE_BLK)[:, None] // HD_A
    c = jnp.arange(GATE_BLK)[None, :] // HD_A

    def bd(w):
        rows = w.reshape(DEPTH, N_GATE_BLK, GATE_BLK, HD_A)
        return jnp.where(r == c, jnp.tile(rows, (1, 1, 1, per)), 0.0)

    return jnp.concatenate([bd(wa), bd(wi)], axis=-1)


def _prep_params(norm_g, w_in, conv_a_w, conv_a_b, lru_wa, lru_ba, lru_wi, lru_bi, lru_lam, conv_b_w,
                 conv_b_b, ln_b_g, ln_b_b, ln_c_g, ln_c_b, gmlp_ws, gmlp_bs, w_out, final_g):
    row = lambda v: v.reshape(DEPTH, 1, -1)
    common = (
        row(norm_g), w_in, w_out, conv_a_w, row(conv_a_b),
        _block_diag_gates(lru_wa, lru_wi).astype(BF16), row(lru_ba), row(lru_bi), row(lru_lam),
        row(conv_b_b), row(ln_b_g), row(ln_b_b), row(ln_c_g), row(ln_c_b),
        final_g.reshape(1, D_MODEL),
    )
    bs_full = jnp.repeat(jnp.transpose(gmlp_bs, (0, 2, 1)), HD_C, axis=2)
    wcol = jnp.repeat(jnp.transpose(gmlp_ws[:, :, :SUBLANES, :SUBLANES], (0, 3, 2, 1)), HD_C, axis=3)
    return {"common": common, "conv_b_w": conv_b_w, "ws": gmlp_ws, "bs_full": bs_full, "wcol": wcol,
            "bs8": bs_full[:, :SUBLANES]}


def kernel(x_prompt, x_sample, c_prompt, c_sample, state_lru_conv, state_lru_h, state_ccm_conv, norm_g, w_ada,
           b_ada, w_in, conv_a_w, conv_a_b, lru_wa, lru_ba, lru_wi, lru_bi, lru_lam, conv_b_w, conv_b_b, ln_b_g,
           ln_b_b, ln_c_g, ln_c_b, gmlp_ws, gmlp_bs, w_out, final_g):
    bp = x_prompt.shape[0]
    mod_p, mod_s = _ada_call(c_prompt, c_sample, w_ada, b_ada)
    p = _prep_params(norm_g, w_in, conv_a_w, conv_a_b, lru_wa, lru_ba, lru_wi, lru_bi, lru_lam, conv_b_w,
                     conv_b_b, ln_b_g, ln_b_b, ln_c_g, ln_c_b, gmlp_ws, gmlp_bs, w_out, final_g)
    hist_a_s = jnp.transpose(state_lru_conv, (0, 2, 1, 3))
    hist_b_s = jnp.transpose(state_ccm_conv, (0, 2, 1, 3))
    xp, xs = x_prompt, x_sample
    conv_a_p, h_p, conv_b_p = [], [], []
    sample_state = None
    for l in range(DEPTH):
        xp, ha, hl, hb, win_bf, wout_bf = _prompt_layer(l, xp, mod_p, p)
        conv_a_p.append(ha); h_p.append(hl.reshape(bp, W_A)); conv_b_p.append(hb)
        xs, sample_state = _sample_layer(l, xs, mod_s, hist_a_s, state_lru_h, hist_b_s, p,
                                         (win_bf, wout_bf), sample_state)
    new_a_s, new_h_s, new_b_s, new_v_s = sample_state
    return (xp, xs,
            jnp.stack(conv_a_p), jnp.stack(h_p), jnp.stack(conv_b_p),
            jnp.transpose(new_a_s, (0, 2, 1, 3)), new_h_s, jnp.transpose(new_b_s, (0, 2, 1, 3)), new_v_s)
```

```python
import functools

import jax
import jax.numpy as jnp
from jax import lax
from jax.experimental import pallas as pl
from jax.experimental.pallas import tpu as pltpu

F32 = jnp.float32
BF16 = jnp.bfloat16

D_MODEL = 2048
DEPTH = 4
MIX = D_MODEL
W_A = MIX // 2
H_A = 16
HD_A = W_A // H_A
CONV_A = 4
C_LRU = 8.0
W_B = MIX // 4
CONV_B = 31
W_C = MIX // 4
H_C = 8
HD_C = W_C // H_C
CHUNK = 128
EPS = 1e-6
IN_COLS = 2 * W_A + 3 * W_B + 3 * W_C

O_XA = 0
O_GA = O_XA + W_A
O_GLA = O_GA + W_A
O_GLB = O_GLA + W_B
O_GB = O_GLB + W_B
O_U = O_GB + W_B
O_V = O_U + W_C
O_GC = O_V + W_C

SUBLANES = 8
LANES = 128
GATE_BLK = 256
N_GATE_BLK = W_A // GATE_BLK
HIST_A = CONV_A - 1
HIST_B = CONV_B - 1
PAD_A = SUBLANES
PAD_B = 32
SLAB_ROWS = 64
N_STAGE = 4
ROW_BLK = 64

PROMPT_ROWS = 256
SAMPLE_SEQS = 32
ADA_COLS = 1024
ADA_FUSED_COLS = 256
VMEM_LIMIT = 58 * 1024 * 1024


def _dot(a, b):
    return jnp.dot(a, b, preferred_element_type=F32)


def _slab_copy(w_hbm, l, s, width, stage, sem):
    slot = s % N_STAGE
    return pltpu.make_async_copy(
        w_hbm.at[l, pl.ds(s * SLAB_ROWS, SLAB_ROWS), :], stage.at[slot, :, 0:width], sem.at[slot])


def _load_cast_weights(w_hbm, l, dst, stage, sem):
    rows, width = dst.shape
    n_slabs = rows // SLAB_ROWS
    ahead = N_STAGE - 1
    assert n_slabs > ahead
    for s in range(ahead):
        _slab_copy(w_hbm, l, s, width, stage, sem).start()

    def body(s, carry):
        @pl.when(s + ahead < n_slabs)
        def _():
            _slab_copy(w_hbm, l, s + ahead, width, stage, sem).start()

        _slab_copy(w_hbm, l, s, width, stage, sem).wait()
        r0 = pl.multiple_of(s * SLAB_ROWS, SLAB_ROWS)
        dst[pl.ds(r0, SLAB_ROWS), :] = stage[s % N_STAGE, :, 0:width].astype(BF16)
        return carry

    lax.fori_loop(0, n_slabs, body, 0)


def _load_layer_weights(l, win_hbm, wout_hbm, win_s, wout_s, stage, sem):
    _load_cast_weights(win_hbm, l, win_s, stage, sem)
    _load_cast_weights(wout_hbm, l, wout_s, stage, sem)


def _handoff_copies(win_src, wout_src, win_dst, wout_dst, sem):
    return (pltpu.make_async_copy(win_src, win_dst, sem.at[0]),
            pltpu.make_async_copy(wout_src, wout_dst, sem.at[1]))


def _sigmoid(x):
    return 0.5 * jnp.tanh(0.5 * x) + 0.5


def _silu(x):
    return x * _sigmoid(x)


def _layernorm(x, g, b):
    mu = jnp.mean(x, axis=-1, keepdims=True)
    xc = x - mu
    return xc * lax.rsqrt(jnp.mean(xc * xc, axis=-1, keepdims=True) + EPS) * g + b


def _rmsnorm(x, g):
    return (x * lax.rsqrt(jnp.mean(x * x, axis=-1, keepdims=True) + EPS)) * g


def _norm_mod(x, g, scale, shift):
    return _rmsnorm(x, g) * (1.0 + scale) + shift


def _lru_coeffs(xac, wg_ref, ba, bi, lam):
    xacb = xac.astype(BF16)
    pa, pi = [], []
    for j in range(N_GATE_BLK):
        g = _dot(xacb[:, j * GATE_BLK:(j + 1) * GATE_BLK], wg_ref[j])
        pa.append(g[:, :GATE_BLK])
        pi.append(g[:, GATE_BLK:])
    r = _sigmoid(jnp.concatenate(pa, axis=1) + ba)
    i = _sigmoid(jnp.concatenate(pi, axis=1) + bi)
    log_a = (-C_LRU * jax.nn.softplus(-lam)) * r
    a = jnp.exp(log_a)
    coef = jnp.sqrt(-jnp.tanh(log_a) * (1.0 + a * a))
    return a, coef * (i * xac)


def _group_scan(a3, b3):
    row = lax.broadcasted_iota(jnp.int32, (1,) + a3.shape[1:], 1)
    for s in (1, 2, 4):
        keep = row >= s
        a_prev = jnp.where(keep, pltpu.roll(a3, s, axis=1), 1.0)
        b_prev = jnp.where(keep, pltpu.roll(b3, s, axis=1), 0.0)
        b3 = a3 * b_prev + b3
        a3 = a3 * a_prev
    return a3, b3


def _tril_mask(n):
    r = lax.broadcasted_iota(jnp.int32, (n, n), 0)
    c = lax.broadcasted_iota(jnp.int32, (n, n), 1)
    return c <= r


def _win(win_ref, off, width):
    return win_ref[:, off:off + width]


def _ada_block(scp, scs, w_ref, b_ref, mp_ref, ms_ref):
    w = w_ref[...].astype(BF16)
    b = b_ref[...]
    mp_ref[...] = _dot(scp, w) + b
    ms_ref[...] = _dot(scs, w) + b


def _ada_kernel(cp_ref, cs_ref, w_ref, b_ref, mp_ref, ms_ref):
    _ada_block(_silu(cp_ref[...]).astype(BF16), _silu(cs_ref[...]).astype(BF16), w_ref, b_ref, mp_ref, ms_ref)


def _ada_call(c_prompt, c_sample, w_ada, b_ada):
    bp, bs = c_prompt.shape[0], c_sample.shape[0]
    n3 = w_ada.shape[-1]
    tn = ADA_COLS
    return pl.pallas_call(
        _ada_kernel,
        grid=(n3 // tn,),
        in_specs=[
            pl.BlockSpec((bp, D_MODEL), lambda j: (0, 0)),
            pl.BlockSpec((bs, D_MODEL), lambda j: (0, 0)),
            pl.BlockSpec((None, D_MODEL, tn), lambda j: (0, 0, j)),
            pl.BlockSpec((None, 1, tn), lambda j: (0, 0, j)),
        ],
        out_specs=[
            pl.BlockSpec((bp, tn), lambda j: (0, j)),
            pl.BlockSpec((bs, tn), lambda j: (0, j)),
        ],
        out_shape=[
            jax.ShapeDtypeStruct((bp, n3), F32),
            jax.ShapeDtypeStruct((bs, n3), F32),
        ],
        compiler_params=pltpu.CompilerParams(
            dimension_semantics=("arbitrary",), vmem_limit_bytes=VMEM_LIMIT),
        name="ada_mod",
    )(c_prompt, c_sample, w_ada, b_ada.reshape(DEPTH, 1, n3))


N_PROMPT_INPUTS = 20
N_ADA_IN = 4
N_ADA_OUT = 2
ADA_PAD_ROWS = 16


def _prompt_kernel(*refs, l, tl, final, with_next_mod):
    n_in = N_PROMPT_INPUTS + (N_ADA_IN if with_next_mod else 0)
    n_out = 6 + (N_ADA_OUT if with_next_mod else 0)
    (x_ref, mod_ref, ng_ref, win_hbm, wout_hbm, caw_ref, cab_ref, wg_ref, ba_ref, bi_ref,
     lam_ref, cbb_ref, lnbg_ref, lnbb_ref, lncg_ref, lncb_ref, fg_ref, cbw_ref, ws_ref,
     bs_ref) = refs[:N_PROMPT_INPUTS]
    xo_ref, hista_ref, hlast_ref, histb_ref, win_bf, wout_bf = refs[n_in:n_in + 6]
    (win_ref, wout_ref, stage, sem, hsem, xa_buf, glu_buf, cs_buf, hc_ref,
     y_buf) = refs[n_in + n_out:n_in + n_out + 10]
    b_idx = pl.program_id(0)
    c = pl.program_id(1)

    @pl.when((b_idx == 0) & (c == 0))
    def _():
        _load_layer_weights(l, win_hbm, wout_hbm, win_ref, wout_ref, stage, sem)
        for cp in _handoff_copies(win_ref, wout_ref, win_bf, wout_bf, hsem):
            cp.start()

    @pl.when((b_idx == pl.num_programs(0) - 1) & (c == pl.num_programs(1) - 1))
    def _():
        for cp in _handoff_copies(win_ref, wout_ref, win_bf, wout_bf, hsem):
            cp.wait()

    @pl.when(c == 0)
    def _():
        xa_buf[0:PAD_A, :] = jnp.zeros((PAD_A, W_A), F32)
        glu_buf[0:PAD_B, :] = jnp.zeros((PAD_B, W_B), F32)
        hc_ref[...] = jnp.zeros_like(hc_ref)

    x = x_ref[0]
    shift = mod_ref[0, :, 0:D_MODEL]
    scale = mod_ref[0, :, D_MODEL:2 * D_MODEL]
    gate = mod_ref[0, :, 2 * D_MODEL:3 * D_MODEL]
    xnb = _norm_mod(x, ng_ref[...], scale, shift).astype(BF16)

    xa_buf[PAD_A:PAD_A + tl, :] = _dot(xnb, _win(win_ref, O_XA, W_A))
    xac = cab_ref[...]
    for k in range(CONV_A):
        o = PAD_A - HIST_A + k
        xac = xac + caw_ref[k:k + 1, :] * xa_buf[o:o + tl, :]
    new_hist_a = xa_buf[PAD_A + tl - HIST_A:PAD_A + tl, :]
    hista_ref[0] = new_hist_a
    xa_buf[PAD_A - HIST_A:PAD_A, :] = new_hist_a
    a, b = _lru_coeffs(xac, wg_ref, ba_ref[...], bi_ref[...], lam_ref[...])

    gla = _dot(xnb, _win(win_ref, O_GLA, W_B))
    glb = _dot(xnb, _win(win_ref, O_GLB, W_B))
    glu_buf[PAD_B:PAD_B + tl, :] = gla * _sigmoid(glb)
    ext = tl + PAD_B - SUBLANES
    for r in range(1, SUBLANES):
        cs_buf[r - 1] = glu_buf[r:r + ext, :]

    ng = tl // SUBLANES
    a3, b3 = _group_scan(a.reshape(ng, SUBLANES, W_A), b.reshape(ng, SUBLANES, W_A))
    carry = hc_ref[...]
    hs = []
    for g in range(ng):
        hg = b3[g] + a3[g] * carry
        carry = hg[SUBLANES - 1:SUBLANES, :]
        hs.append(hg)
    h = jnp.concatenate(hs, axis=0)
    hc_ref[...] = carry
    hlast_ref[0] = carry
    ga = _dot(xnb, _win(win_ref, O_GA, W_A))
    y_buf[:, 0:W_A] = (h * _silu(ga)).astype(BF16)

    gb = _dot(xnb, _win(win_ref, O_GB, W_B))
    for rb in range(tl // ROW_BLK):
        r0 = rb * ROW_BLK
        cb = cbb_ref[...]
        for k in range(CONV_B):
            q, r = divmod(k + PAD_B - HIST_B, SUBLANES)
            o = r0 + q * SUBLANES
            src = glu_buf[o:o + ROW_BLK, :] if r == 0 else cs_buf[r - 1, o:o + ROW_BLK, :]
            cb = cb + cbw_ref[k:k + 1, :] * src
        yb = _silu(_layernorm(cb, lnbg_ref[...], lnbb_ref[...])) * _silu(gb[r0:r0 + ROW_BLK, :])
        y_buf[r0:r0 + ROW_BLK, W_A:W_A + W_B] = yb.astype(BF16)
    new_hist_b = glu_buf[PAD_B + tl - HIST_B:PAD_B + tl, :]
    histb_ref[0] = new_hist_b
    glu_buf[PAD_B - HIST_B:PAD_B, :] = new_hist_b

    v = _dot(xnb, _win(win_ref, O_V, W_C))
    vnb = _layernorm(v, lncg_ref[...], lncb_ref[...]).astype(BF16)
    tri = _tril_mask(CHUNK)
    lane = lax.broadcasted_iota(jnp.int32, (CHUNK, LANES), 1)
    lo_half = lane < HD_C
    wpair = []
    for p in range(H_C // 2):
        w0 = jnp.where(tri, ws_ref[2 * p], 0.0).astype(BF16)
        w1 = jnp.where(tri, ws_ref[2 * p + 1], 0.0).astype(BF16)
        wpair.append(jnp.concatenate([w0, w1], axis=1))
    zero = jnp.zeros((CHUNK, LANES), BF16)
    chunks = []
    for cc in range(tl // CHUNK):
        cols = []
        for p in range(H_C // 2):
            vp = vnb[cc * CHUNK:(cc + 1) * CHUNK, p * LANES:(p + 1) * LANES]
            rhs = jnp.concatenate([jnp.where(lo_half, vp, zero), jnp.where(lo_half, zero, vp)], axis=0)
            cols.append(_dot(wpair[p], rhs))
        chunks.append(jnp.concatenate(cols, axis=1) + bs_ref[...])
    mixed = jnp.concatenate(chunks, axis=0)
    u = _dot(xnb, _win(win_ref, O_U, W_C))
    gc = _dot(xnb, _win(win_ref, O_GC, W_C))
    y_buf[:, W_A + W_B:MIX] = (u * mixed * _silu(gc)).astype(BF16)

    out = _dot(y_buf[...], wout_ref[...])
    xnew = x + gate * out
    if final:
        xnew = _rmsnorm(xnew, fg_ref[...])
    xo_ref[0] = xnew

    if with_next_mod:
        cp_ref, cs_ref, wada_ref, bada_ref = refs[N_PROMPT_INPUTS:n_in]
        mpn_ref, msn_ref = refs[n_in + 6:n_in + n_out]
        (sc,) = refs[n_in + n_out + 10:]
        bp, bs = cp_ref.shape[0], cs_ref.shape[0]
        step = b_idx * pl.num_programs(1) + c

        @pl.when(step == 0)
        def _():
            sc[...] = jnp.zeros_like(sc)
            sc[0:bs, :] = _silu(cs_ref[...]).astype(BF16)
            sc[bs:bs + bp, :] = _silu(cp_ref[...]).astype(BF16)

        @pl.when(step < (3 * D_MODEL) // ADA_FUSED_COLS)
        def _():
            r = _dot(sc[...], wada_ref[...].astype(BF16))
            msn_ref[...] = r[0:bs, :] + bada_ref[...]
            mpn_ref[...] = r[bs:bs + bp, :] + bada_ref[...]


def _layer_spec(l, shape):
    nd = len(shape)
    return pl.BlockSpec((None,) + tuple(shape), lambda *_: (l,) + (0,) * nd, pipeline_mode=pl.Buffered(1))


def _const_spec(shape):
    nd = len(shape)
    return pl.BlockSpec(tuple(shape), lambda *_: (0,) * nd, pipeline_mode=pl.Buffered(1))


def _layer_param_specs(l):
    return [
        _layer_spec(l, (1, D_MODEL)),
        pl.BlockSpec(memory_space=pl.ANY),
        pl.BlockSpec(memory_space=pl.ANY),
        _layer_spec(l, (CONV_A, W_A)),
        _layer_spec(l, (1, W_A)),
        _layer_spec(l, (N_GATE_BLK, GATE_BLK, 2 * GATE_BLK)),
        _layer_spec(l, (1, W_A)),
        _layer_spec(l, (1, W_A)),
        _layer_spec(l, (1, W_A)),
        _layer_spec(l, (1, W_B)),
        _layer_spec(l, (1, W_B)),
        _layer_spec(l, (1, W_B)),
        _layer_spec(l, (1, W_C)),
        _layer_spec(l, (1, W_C)),
        _const_spec((1, D_MODEL)),
    ]


def _prompt_layer(l, x, mod_p, p, ada):
    bp, seq, _ = x.shape
    tl = PROMPT_ROWS
    nc = seq // tl
    with_next_mod = l < DEPTH - 1
    kern = functools.partial(_prompt_kernel, l=l, tl=tl, final=(l == DEPTH - 1), with_next_mod=with_next_mod)
    in_specs = [
        pl.BlockSpec((1, tl, D_MODEL), lambda b, c: (b, c, 0)),
        pl.BlockSpec((1, 1, 3 * D_MODEL), lambda b, c: (b, 0, 0)),
    ] + _layer_param_specs(l) + [
        _layer_spec(l, (CONV_B, W_B)),
        _layer_spec(l, (H_C, CHUNK, CHUNK)),
        _layer_spec(l, (CHUNK, W_C)),
    ]
    args = [x, mod_p.reshape(bp, 1, 3 * D_MODEL), *p["common"], p["conv_b_w"], p["ws"], p["bs_full"]]
    assert len(args) == N_PROMPT_INPUTS
    ada_out_specs, ada_out_shape, ada_scratch = [], [], []
    if with_next_mod:
        c_prompt, c_sample, w_ada, b_ada = ada
        bs = c_sample.shape[0]
        n3 = w_ada.shape[-1]
        tn = ADA_FUSED_COLS
        n_blk = n3 // tn
        assert n_blk <= bp * nc
        blk = lambda b, c: jnp.minimum(b * nc + c, n_blk - 1)
        in_specs += [
            _const_spec((bp, D_MODEL)),
            _const_spec((bs, D_MODEL)),
            pl.BlockSpec((None, D_MODEL, tn), lambda b, c: (l + 1, 0, blk(b, c))),
            pl.BlockSpec((None, 1, tn), lambda b, c: (l + 1, 0, blk(b, c))),
        ]
        args += [c_prompt, c_sample, w_ada, b_ada.reshape(DEPTH, 1, n3)]
        ada_out_specs = [pl.BlockSpec((bp, tn), lambda b, c: (0, blk(b, c))),
                         pl.BlockSpec((bs, tn), lambda b, c: (0, blk(b, c)))]
        ada_out_shape = [jax.ShapeDtypeStruct((bp, n3), F32), jax.ShapeDtypeStruct((bs, n3), F32)]
        ada_scratch = [pltpu.VMEM((bs + ADA_PAD_ROWS, D_MODEL), BF16)]
        assert bp <= ADA_PAD_ROWS and bs % ADA_PAD_ROWS == 0
    out_specs = [
        pl.BlockSpec((1, tl, D_MODEL), lambda b, c: (b, c, 0)),
        pl.BlockSpec((1, HIST_A, W_A), lambda b, c: (b, 0, 0)),
        pl.BlockSpec((1, 1, W_A), lambda b, c: (b, 0, 0)),
        pl.BlockSpec((1, HIST_B, W_B), lambda b, c: (b, 0, 0)),
        pl.BlockSpec(memory_space=pl.ANY),
        pl.BlockSpec(memory_space=pl.ANY),
    ] + ada_out_specs
    out_shape = [
        jax.ShapeDtypeStruct((bp, seq, D_MODEL), F32),
        jax.ShapeDtypeStruct((bp, HIST_A, W_A), F32),
        jax.ShapeDtypeStruct((bp, 1, W_A), F32),
        jax.ShapeDtypeStruct((bp, HIST_B, W_B), F32),
        jax.ShapeDtypeStruct((D_MODEL, IN_COLS), BF16),
        jax.ShapeDtypeStruct((MIX, D_MODEL), BF16),
    ] + ada_out_shape
    return pl.pallas_call(
        kern,
        grid=(bp, nc),
        in_specs=in_specs,
        out_specs=out_specs,
        out_shape=out_shape,
        scratch_shapes=[
            pltpu.VMEM((D_MODEL, IN_COLS), BF16),
            pltpu.VMEM((MIX, D_MODEL), BF16),
            pltpu.VMEM((N_STAGE, SLAB_ROWS, IN_COLS), F32),
            pltpu.SemaphoreType.DMA((N_STAGE,)),
            pltpu.SemaphoreType.DMA((2,)),
            pltpu.VMEM((PAD_A + tl, W_A), F32),
            pltpu.VMEM((PAD_B + tl, W_B), F32),
            pltpu.VMEM((SUBLANES - 1, tl + PAD_B - SUBLANES, W_B), F32),
            pltpu.VMEM((1, W_A), F32),
            pltpu.VMEM((tl, MIX), BF16),
        ] + ada_scratch,
        compiler_params=pltpu.CompilerParams(
            dimension_semantics=("arbitrary", "arbitrary"), vmem_limit_bytes=VMEM_LIMIT),
        name="prompt_layer",
    )(*args)


N_SAMPLE_INPUTS = 23


def _sample_kernel(*refs, nb, final, n_alias):
    (x_ref, mod_ref, ng_ref, win_hbm, wout_hbm, caw_ref, cab_ref, wg_ref, ba_ref, bi_ref,
     lam_ref, cbb_ref, lnbg_ref, lnbb_ref, lncg_ref, lncb_ref, fg_ref,
     cbw_ref, wcol_ref, bs8_ref, hista_in, h0_in, histb_in) = refs[:N_SAMPLE_INPUTS]
    (xo_ref, hista_out, hlast_out, histb_out, vn_out,
     win_ref, wout_ref, sem, y_buf) = refs[N_SAMPLE_INPUTS + n_alias:]

    @pl.when(pl.program_id(0) == 0)
    def _():
        copies = _handoff_copies(win_hbm, wout_hbm, win_ref, wout_ref, sem)
        for cp in copies:
            cp.start()
        for cp in copies:
            cp.wait()

    nt = SUBLANES

    def blk(v, s):
        return v[s * nb:(s + 1) * nb, :]

    def per_token(v):
        return jnp.concatenate([v] * nt, axis=0)

    x = jnp.concatenate([x_ref[:, s, :] for s in range(nt)], axis=0)
    shift = per_token(mod_ref[:, 0:D_MODEL])
    scale = per_token(mod_ref[:, D_MODEL:2 * D_MODEL])
    xnb = _norm_mod(x, ng_ref[...], scale, shift).astype(BF16)

    xa = _dot(xnb, _win(win_ref, O_XA, W_A))
    xc = [hista_in[j] for j in range(HIST_A)] + [blk(xa, s) for s in range(nt)]
    xac = []
    for s in range(nt):
        acc = cab_ref[...]
        for k in range(CONV_A):
            acc = acc + caw_ref[k:k + 1, :] * xc[s + k]
        xac.append(acc)
    for k in range(HIST_A):
        hista_out[k] = xc[nt + k]
    a, b = _lru_coeffs(jnp.concatenate(xac, axis=0), wg_ref, ba_ref[...], bi_ref[...], lam_ref[...])
    h = h0_in[...]
    hs = []
    for s in range(nt):
        h = blk(a, s) * h + blk(b, s)
        hs.append(h)
    hlast_out[...] = h
    ga = _dot(xnb, _win(win_ref, O_GA, W_A))
    y_buf[:, 0:W_A] = (jnp.concatenate(hs, axis=0) * _silu(ga)).astype(BF16)

    gla = _dot(xnb, _win(win_ref, O_GLA, W_B))
    glb = _dot(xnb, _win(win_ref, O_GLB, W_B))
    glu = gla * _sigmoid(glb)
    xcb = [histb_in[j] for j in range(HIST_B)] + [blk(glu, s) for s in range(nt)]
    cb = []
    for s in range(nt):
        acc = cbb_ref[...]
        for k in range(CONV_B):
            acc = acc + cbw_ref[k:k + 1, :] * xcb[s + k]
        cb.append(acc)
    histb_out[0:HIST_B - nt] = histb_in[nt:HIST_B]
    for s in range(nt):
        histb_out[HIST_B - nt + s] = blk(glu, s)
    gb = _dot(xnb, _win(win_ref, O_GB, W_B))
    yb = _silu(_layernorm(jnp.concatenate(cb, axis=0), lnbg_ref[...], lnbb_ref[...])) * _silu(gb)
    y_buf[:, W_A:W_A + W_B] = yb.astype(BF16)

    v = _dot(xnb, _win(win_ref, O_V, W_C))
    vn = _layernorm(v, lncg_ref[...], lncb_ref[...])
    for s in range(nt):
        vn_out[:, s, :] = blk(vn, s)
    mixed = []
    for tt in range(nt):
        acc = bs8_ref[tt:tt + 1, :]
        for s in range(tt + 1):
            acc = acc + wcol_ref[s, tt:tt + 1, :] * blk(vn, s)
        mixed.append(acc)
    u = _dot(xnb, _win(win_ref, O_U, W_C))
    gc = _dot(xnb, _win(win_ref, O_GC, W_C))
    y_buf[:, W_A + W_B:MIX] = (u * jnp.concatenate(mixed, axis=0) * _silu(gc)).astype(BF16)

    out = _dot(y_buf[...], wout_ref[...])
    xnew = x + per_token(mod_ref[:, 2 * D_MODEL:3 * D_MODEL]) * out
    if final:
        xnew = _rmsnorm(xnew, fg_ref[...])
    for s in range(nt):
        xo_ref[:, s, :] = blk(xnew, s)


def _sample_layer(l, x, mod_s, hist_a, h0, hist_b, p, wbf, prev):
    bs, ls, _ = x.shape
    assert ls == SUBLANES
    nb = SAMPLE_SEQS
    out_shape = [
        jax.ShapeDtypeStruct((bs, SUBLANES, D_MODEL), F32),
        jax.ShapeDtypeStruct((DEPTH, HIST_A, bs, W_A), F32),
        jax.ShapeDtypeStruct((DEPTH, bs, W_A), F32),
        jax.ShapeDtypeStruct((DEPTH, HIST_B, bs, W_B), F32),
        jax.ShapeDtypeStruct((DEPTH, bs, SUBLANES, W_C), F32),
    ]
    if prev is None:
        prev = tuple(jnp.zeros(s.shape, s.dtype) for s in out_shape[1:])
    n_alias = len(prev)
    kern = functools.partial(_sample_kernel, nb=nb, final=(l == DEPTH - 1), n_alias=n_alias)
    in_specs = [
        pl.BlockSpec((nb, SUBLANES, D_MODEL), lambda i: (i, 0, 0)),
        pl.BlockSpec((nb, 3 * D_MODEL), lambda i: (i, 0)),
    ] + _layer_param_specs(l) + [
        _layer_spec(l, (CONV_B, W_B)),
        _layer_spec(l, (SUBLANES, SUBLANES, W_C)),
        _layer_spec(l, (SUBLANES, W_C)),
        pl.BlockSpec((None, HIST_A, nb, W_A), lambda i: (l, 0, i, 0)),
        pl.BlockSpec((None, nb, W_A), lambda i: (l, i, 0)),
        pl.BlockSpec((None, HIST_B, nb, W_B), lambda i: (l, 0, i, 0)),
    ] + [pl.BlockSpec(memory_space=pl.ANY)] * n_alias
    out_specs = [
        pl.BlockSpec((nb, SUBLANES, D_MODEL), lambda i: (i, 0, 0)),
        pl.BlockSpec((None, HIST_A, nb, W_A), lambda i: (l, 0, i, 0)),
        pl.BlockSpec((None, nb, W_A), lambda i: (l, i, 0)),
        pl.BlockSpec((None, HIST_B, nb, W_B), lambda i: (l, 0, i, 0)),
        pl.BlockSpec((None, nb, SUBLANES, W_C), lambda i: (l, i, 0, 0)),
    ]
    common = list(p["common"])
    common[1:3] = wbf
    args = [x, mod_s, *common, p["conv_b_w"], p["wcol"], p["bs8"], hist_a, h0, hist_b]
    assert len(args) == N_SAMPLE_INPUTS
    aliases = {len(args) + k: 1 + k for k in range(n_alias)}
    args += list(prev)
    outs = pl.pallas_call(
        kern,
        grid=(bs // nb,),
        in_specs=in_specs,
        out_specs=out_specs,
        out_shape=out_shape,
        input_output_aliases=aliases,
        scratch_shapes=[
            pltpu.VMEM((D_MODEL, IN_COLS), BF16),
            pltpu.VMEM((MIX, D_MODEL), BF16),
            pltpu.SemaphoreType.DMA((2,)),
            pltpu.VMEM((nb * SUBLANES, MIX), BF16),
        ],
        compiler_params=pltpu.CompilerParams(
            dimension_semantics=("arbitrary",), vmem_limit_bytes=VMEM_LIMIT),
        name="sample_layer",
    )(*args)
    return outs[0], tuple(outs[1:])


def _block_diag_gates(wa, wi):
    per = GATE_BLK // HD_A
    r = jnp.arange(GATE_BLK)[:, None] // HD_A
    c = jnp.arange(GATE_BLK)[None, :] // HD_A

    def bd(w):
        rows = w.reshape(DEPTH, N_GATE_BLK, GATE_BLK, HD_A)
        return jnp.where(r == c, jnp.tile(rows, (1, 1, 1, per)), 0.0)

    return jnp.concatenate([bd(wa), bd(wi)], axis=-1)


def _prep_params(norm_g, w_in, conv_a_w, conv_a_b, lru_wa, lru_ba, lru_wi, lru_bi, lru_lam, conv_b_w,
                 conv_b_b, ln_b_g, ln_b_b, ln_c_g, ln_c_b, gmlp_ws, gmlp_bs, w_out, final_g):
    row = lambda v: v.reshape(DEPTH, 1, -1)
    common = (
        row(norm_g), w_in, w_out, conv_a_w, row(conv_a_b),
        _block_diag_gates(lru_wa, lru_wi).astype(BF16), row(lru_ba), row(lru_bi), row(lru_lam),
        row(conv_b_b), row(ln_b_g), row(ln_b_b), row(ln_c_g), row(ln_c_b),
        final_g.reshape(1, D_MODEL),
    )
    bs_full = jnp.repeat(jnp.transpose(gmlp_bs, (0, 2, 1)), HD_C, axis=2)
    wcol = jnp.repeat(jnp.transpose(gmlp_ws[:, :, :SUBLANES, :SUBLANES], (0, 3, 2, 1)), HD_C, axis=3)
    return {"common": common, "conv_b_w": conv_b_w, "ws": gmlp_ws, "bs_full": bs_full, "wcol": wcol,
            "bs8": bs_full[:, :SUBLANES]}


def kernel(x_prompt, x_sample, c_prompt, c_sample, state_lru_conv, state_lru_h, state_ccm_conv, norm_g, w_ada,
           b_ada, w_in, conv_a_w, conv_a_b, lru_wa, lru_ba, lru_wi, lru_bi, lru_lam, conv_b_w, conv_b_b, ln_b_g,
           ln_b_b, ln_c_g, ln_c_b, gmlp_ws, gmlp_bs, w_out, final_g):
    bp = x_prompt.shape[0]
    mod_p, mod_s = _ada_call(c_prompt, c_sample, w_ada, b_ada)
    p = _prep_params(norm_g, w_in, conv_a_w, conv_a_b, lru_wa, lru_ba, lru_wi, lru_bi, lru_lam, conv_b_w,
                     conv_b_b, ln_b_g, ln_b_b, ln_c_g, ln_c_b, gmlp_ws, gmlp_bs, w_out, final_g)
    hist_a_s = jnp.transpose(state_lru_conv, (0, 2, 1, 3))
    hist_b_s = jnp.transpose(state_ccm_conv, (0, 2, 1, 3))
    xp, xs = x_prompt, x_sample
    conv_a_p, h_p, conv_b_p = [], [], []
    sample_state = None
    for l in range(DEPTH):
        outs = _prompt_layer(l, xp, mod_p, p, (c_prompt, c_sample, w_ada, b_ada))
        xp, ha, hl, hb, win_bf, wout_bf = outs[:6]
        conv_a_p.append(ha); h_p.append(hl.reshape(bp, W_A)); conv_b_p.append(hb)
        xs, sample_state = _sample_layer(l, xs, mod_s, hist_a_s, state_lru_h, hist_b_s, p,
                                         (win_bf, wout_bf), sample_state)
        if len(outs) > 6:
            mod_p, mod_s = outs[6:]
    new_a_s, new_h_s, new_b_s, new_v_s = sample_state
    return (xp, xs,
            jnp.stack(conv_a_p), jnp.stack(h_p), jnp.stack(conv_b_p),
            jnp.transpose(new_a_s, (0, 2, 1, 3)), new_h_s, jnp.transpose(new_b_s, (0, 2, 1, 3)), new_v_s)
```
